```python
import math
import jax, jax.numpy as jnp
from jax import lax
import numpy as np

D_MODEL = 1024
BATCH = 16
SEQ = 2048
DEPTH = 2

CHUNK = 64
Q_BLOCK = 128
HEAD_DIM = 64
D_MIX = D_MODEL
SB_HEADS = (D_MIX // 2) // HEAD_DIM
SB_WIDTH = SB_HEADS * HEAD_DIM
DA_VDIM = 2 * HEAD_DIM
DA_HEADS = (D_MIX // 2) // DA_VDIM
DA_QK_WIDTH = DA_HEADS * 2 * HEAD_DIM
DA_WIDTH = DA_HEADS * DA_VDIM
PROJ_WIDTH = 4 * SB_WIDTH + 2 * DA_QK_WIDTH + 2 * DA_WIDTH
RMS_EPS = 1e-6

kernel_name = 'hybrid_stickbreak_diffattn_block'


def _rmsnorm(x, g):
    xf = x.astype(jnp.float32)
    y = xf * lax.rsqrt(jnp.mean(xf * xf, axis=-1, keepdims=True) + RMS_EPS)
    return (y * g.astype(jnp.float32)).astype(x.dtype)


def _lambda_init(layer_idx):
    return 0.8 - 0.6 * math.exp(-0.3 * layer_idx)


def _stick_breaking(q, k, v):
    S, Dh = q.shape[1], q.shape[-1]
    scale = Dh ** -0.5
    outs = []
    for i in range(S // Q_BLOCK):
        q0 = i * Q_BLOCK
        kend = q0 + Q_BLOCK
        z = jnp.einsum('bqhd,bkhd->bhqk', q[:, q0:kend], k[:, :kend]).astype(jnp.float32) * scale
        tpos = q0 + jnp.arange(Q_BLOCK)[:, None]
        spos = jnp.arange(kend)[None, :]
        mask = spos < tpos
        log_beta = jax.nn.log_sigmoid(z)
        log_1m = jnp.where(mask, jax.nn.log_sigmoid(-z), 0.0)
        tail = lax.cumsum(log_1m, axis=3, reverse=True) - log_1m
        a = jnp.where(mask, jnp.exp(log_beta + tail), 0.0)
        outs.append(jnp.einsum('bhqk,bkhd->bqhd', a.astype(v.dtype), v[:, :kend]))
    return jnp.concatenate(outs, axis=1)


def _diff_attention(q, k, v, lam, slopes):
    S, Dh = q.shape[1], q.shape[-1]
    scale = Dh ** -0.5
    outs = []
    for i in range(S // Q_BLOCK):
        q0 = i * Q_BLOCK
        kend = q0 + Q_BLOCK
        s12 = jnp.einsum('bqhcd,bkhcd->bhcqk', q[:, q0:kend], k[:, :kend]).astype(jnp.float32) * scale
        tpos = q0 + jnp.arange(Q_BLOCK)[:, None]
        spos = jnp.arange(kend)[None, :]
        mask = (spos // CHUNK) <= (tpos // CHUNK)
        dist = jnp.abs(tpos - spos).astype(jnp.float32)
        alibi = -slopes[:, None, None] * dist[None]
        logits = jnp.where(mask, s12 + alibi[None, :, None], -jnp.inf)
        p = jax.nn.softmax(logits, axis=-1)
        a = p[:, :, 0] - lam * p[:, :, 1]
        outs.append(jnp.einsum('bhqk,bkhe->bqhe', a.astype(v.dtype), v[:, :kend]))
    return jnp.concatenate(outs, axis=1)


def _layer(x, norm_g, w_in, w_out, q_norm_g, k_norm_g, lq1, lk1, lq2, lk2, subln_g, layer_idx):
    B, S, _ = x.shape
    h = _rmsnorm(x, norm_g)
    proj = h @ w_in
    sizes = [SB_WIDTH] * 4 + [DA_QK_WIDTH, DA_QK_WIDTH, DA_WIDTH, DA_WIDTH]
    offsets = [int(o) for o in np.cumsum(sizes)[:-1]]
    sb_q, sb_k, sb_v, sb_g, da_q, da_k, da_v, da_g = jnp.split(proj, offsets, axis=-1)

    shp = (B, S, SB_HEADS, HEAD_DIM)
    sb_o = _stick_breaking(sb_q.reshape(shp), sb_k.reshape(shp), sb_v.reshape(shp))
    sb_o = sb_o.reshape(B, S, SB_WIDTH) * jax.nn.silu(sb_g)

    qk_shp = (B, S, DA_HEADS, 2, HEAD_DIM)
    dq = _rmsnorm(da_q.reshape(qk_shp), q_norm_g)
    dk = _rmsnorm(da_k.reshape(qk_shp), k_norm_g)
    dv = da_v.reshape(B, S, DA_HEADS, DA_VDIM)
    lam_init = _lambda_init(layer_idx)
    lam = (jnp.exp(jnp.sum(lq1.astype(jnp.float32) * lk1.astype(jnp.float32)))
           - jnp.exp(jnp.sum(lq2.astype(jnp.float32) * lk2.astype(jnp.float32))) + lam_init)
    slopes = jnp.asarray(2.0 ** (-8.0 * np.arange(1, DA_HEADS + 1) / DA_HEADS), dtype=jnp.float32)
    da_o = _diff_attention(dq, dk, dv, lam, slopes)
    da_o = _rmsnorm(da_o, subln_g) * (1.0 - lam_init)
    da_o = da_o.reshape(B, S, DA_WIDTH) * jax.nn.silu(da_g)

    mixed = jnp.concatenate([sb_o, da_o], axis=-1)
    return x + mixed @ w_out


def setup_inputs(seed: int = 0) -> dict:
    key = jax.random.key(seed)
    ks = jax.random.split(key, 12)
    f32 = jnp.float32
    x = jax.random.normal(ks[0], (BATCH, SEQ, D_MODEL), f32)
    norm_g = 1.0 + 0.02 * jax.random.normal(ks[1], (DEPTH, D_MODEL), f32)
    w_in = jax.random.normal(ks[2], (DEPTH, D_MODEL, PROJ_WIDTH), f32) * D_MODEL ** -0.5
    w_out = jax.random.normal(ks[3], (DEPTH, D_MIX, D_MODEL), f32) * (D_MIX ** -0.5) * 0.5
    q_norm_g = 1.0 + 0.02 * jax.random.normal(ks[4], (DEPTH, HEAD_DIM), f32)
    k_norm_g = 1.0 + 0.02 * jax.random.normal(ks[5], (DEPTH, HEAD_DIM), f32)
    lambda_q1 = 0.1 * jax.random.normal(ks[6], (DEPTH, HEAD_DIM), f32)
    lambda_k1 = 0.1 * jax.random.normal(ks[7], (DEPTH, HEAD_DIM), f32)
    lambda_q2 = 0.1 * jax.random.normal(ks[8], (DEPTH, HEAD_DIM), f32)
    lambda_k2 = 0.1 * jax.random.normal(ks[9], (DEPTH, HEAD_DIM), f32)
    subln_g = 1.0 + 0.02 * jax.random.normal(ks[10], (DEPTH, DA_VDIM), f32)
    return {'x': x, 'norm_g': norm_g, 'w_in': w_in, 'w_out': w_out,
            'q_norm_g': q_norm_g, 'k_norm_g': k_norm_g,
            'lambda_q1': lambda_q1, 'lambda_k1': lambda_k1,
            'lambda_q2': lambda_q2, 'lambda_k2': lambda_k2, 'subln_g': subln_g}


def reference(x, norm_g, w_in, w_out, q_norm_g, k_norm_g, lambda_q1, lambda_k1, lambda_q2, lambda_k2, subln_g):
    for l in range(DEPTH):
        x = _layer(x, norm_g[l], w_in[l], w_out[l], q_norm_g[l], k_norm_g[l],
                   lambda_q1[l], lambda_k1[l], lambda_q2[l], lambda_k2[l], subln_g[l], l)
    return x
```

```python
import functools
import math

import jax
import jax.numpy as jnp
from jax import lax
from jax.experimental import pallas as pl
from jax.experimental.pallas import tpu as pltpu

D_MODEL = 1024
CHUNK = 64
HEAD_DIM = 64
SB_WIDTH = 512
DA_HEADS = 4
DA_WIDTH = 512
PROJ_WIDTH = 4096
RMS_EPS = 1e-6
LANES = 128
NEG_BIG = -1e30

TM_PROJ = 512
TQ = 256
TK = 256
VMEM_LIMIT = 56 * 1024 * 1024

_NT = (((1,), (1,)), ((), ()))


def _dot(a, b):
    return jnp.dot(a, b, preferred_element_type=jnp.float32)


def _dot_nt(a, b):
    return lax.dot_general(a, b, _NT, preferred_element_type=jnp.float32)


def _norm_proj_kernel(x_ref, g_ref, w_ref, qg_ref, kg_ref, qkv_ref, gate_ref, grp_ref):
    @pl.when(pl.program_id(0) == 0)
    def _():
        r = lax.broadcasted_iota(jnp.int32, (SB_WIDTH, SB_WIDTH), 0) // HEAD_DIM
        c = lax.broadcasted_iota(jnp.int32, (SB_WIDTH, SB_WIDTH), 1) // HEAD_DIM
        grp_ref[...] = jnp.where(r == c, 1.0, 0.0).astype(jnp.bfloat16)

    x = x_ref[...]
    ms = jnp.mean(x * x, axis=-1, keepdims=True)
    h = (x * lax.rsqrt(ms + RMS_EPS) * g_ref[...]).astype(jnp.bfloat16)

    def chunk(c):
        return _dot(h, w_ref[:, c * SB_WIDTH:(c + 1) * SB_WIDTH])

    def head_norm(y, gain):
        y2 = y * y
        hi = y2.astype(jnp.bfloat16)
        lo = (y2 - hi.astype(jnp.float32)).astype(jnp.bfloat16)
        ss = _dot(hi, grp_ref[...]) + _dot(lo, grp_ref[...])
        return y * lax.rsqrt(ss * (1.0 / HEAD_DIM) + RMS_EPS) * gain

    qkv_ref[:, 0 * SB_WIDTH:1 * SB_WIDTH] = chunk(0).astype(jnp.bfloat16)
    qkv_ref[:, 1 * SB_WIDTH:2 * SB_WIDTH] = chunk(1).astype(jnp.bfloat16)
    qkv_ref[:, 2 * SB_WIDTH:3 * SB_WIDTH] = chunk(2).astype(jnp.bfloat16)
    gate_ref[:, 0:SB_WIDTH] = chunk(3)
    qkv_ref[:, 3 * SB_WIDTH:4 * SB_WIDTH] = head_norm(chunk(4), qg_ref[...]).astype(jnp.bfloat16)
    qkv_ref[:, 4 * SB_WIDTH:5 * SB_WIDTH] = head_norm(chunk(5), kg_ref[...]).astype(jnp.bfloat16)
    qkv_ref[:, 5 * SB_WIDTH:6 * SB_WIDTH] = chunk(6).astype(jnp.bfloat16)
    gate_ref[:, SB_WIDTH:2 * SB_WIDTH] = chunk(7)


def _norm_proj(x2d, norm_g, w_bf16, qg_t, kg_t):
    m = x2d.shape[0]
    return pl.pallas_call(
        _norm_proj_kernel,
        grid=(m // TM_PROJ,),
        in_specs=[
            pl.BlockSpec((TM_PROJ, D_MODEL), lambda i: (i, 0)),
            pl.BlockSpec((1, D_MODEL), lambda i: (0, 0)),
            pl.BlockSpec((D_MODEL, PROJ_WIDTH), lambda i: (0, 0)),
            pl.BlockSpec((1, SB_WIDTH), lambda i: (0, 0)),
            pl.BlockSpec((1, SB_WIDTH), lambda i: (0, 0)),
        ],
        out_specs=[
            pl.BlockSpec((TM_PROJ, 6 * SB_WIDTH), lambda i: (i, 0)),
            pl.BlockSpec((TM_PROJ, 2 * SB_WIDTH), lambda i: (i, 0)),
        ],
        out_shape=[
            jax.ShapeDtypeStruct((m, 6 * SB_WIDTH), jnp.bfloat16),
            jax.ShapeDtypeStruct((m, 2 * SB_WIDTH), jnp.float32),
        ],
        scratch_shapes=[pltpu.VMEM((SB_WIDTH, SB_WIDTH), jnp.bfloat16)],
        compiler_params=pltpu.CompilerParams(
            dimension_semantics=("arbitrary",), vmem_limit_bytes=VMEM_LIMIT),
        name="norm_proj",
    )(x2d, norm_g, w_bf16, qg_t, kg_t)


def _sb_kernel(q_ref, k_ref, v_ref, g_ref, o_ref, acc_ref, car_ref, tri_ref):
    qi = pl.program_id(2)

    @pl.when((pl.program_id(0) == 0) & (pl.program_id(1) == 0) & (qi == 0))
    def _():
        r = lax.broadcasted_iota(jnp.int32, (TK, TK), 0)
        c = lax.broadcasted_iota(jnp.int32, (TK, TK), 1)
        tri_ref[...] = jnp.where(r > c, 1.0, 0.0).astype(jnp.bfloat16)

    lane = lax.broadcasted_iota(jnp.int32, (1, LANES), 1)
    q = q_ref[0] * jnp.asarray(HEAD_DIM ** -0.5, jnp.bfloat16)
    zero = jnp.zeros_like(q)
    q_heads = (jnp.where(lane < HEAD_DIM, q, zero), jnp.where(lane >= HEAD_DIM, q, zero))

    acc_ref[...] = jnp.zeros_like(acc_ref)
    car_ref[...] = jnp.zeros_like(car_ref)

    def block(j, masked):
        start = pl.multiple_of(j * TK, TK)
        k = k_ref[0, pl.ds(start, TK), :]
        v = v_ref[0, pl.ds(start, TK), :]
        if masked:
            row = lax.broadcasted_iota(jnp.int32, (TQ, TK), 0)
            col = lax.broadcasted_iota(jnp.int32, (TQ, TK), 1)
            mask = col < row
        for h in range(2):
            z = _dot_nt(q_heads[h], k)
            log1m = jnp.minimum(-z, 0.0) - jnp.log(1.0 + jnp.exp(-jnp.abs(z)))
            if masked:
                log1m = jnp.where(mask, log1m, 0.0)
            tail = _dot(log1m.astype(jnp.bfloat16), tri_ref[...])
            a = jnp.exp(log1m + z + tail + car_ref[h])
            if masked:
                a = jnp.where(mask, a, 0.0)
            acc_ref[h] += _dot(a.astype(jnp.bfloat16), v)
            car_ref[h] += jnp.sum(log1m, axis=-1, keepdims=True)

    block(qi, True)

    def body(it, carry):
        block(qi - 1 - it, False)
        return carry

    lax.fori_loop(0, qi, body, 0)

    out = jnp.where(lane < HEAD_DIM, acc_ref[0], acc_ref[1])
    g = g_ref[0]
    o_ref[0] = (out * (g * jax.nn.sigmoid(g))).astype(o_ref.dtype)


def _sb_attention(qkv, gates, batch, seq):
    nq = seq // TQ
    hp = SB_WIDTH // LANES
    return pl.pallas_call(
        _sb_kernel,
        grid=(batch, hp, nq),
        in_specs=[
            pl.BlockSpec((1, TQ, LANES), lambda b, p, i: (b, i, p)),
            pl.BlockSpec((1, seq, LANES), lambda b, p, i: (b, 0, hp + p)),
            pl.BlockSpec((1, seq, LANES), lambda b, p, i: (b, 0, 2 * hp + p)),
            pl.BlockSpec((1, TQ, LANES), lambda b, p, i: (b, i, p)),
        ],
        out_specs=pl.BlockSpec((1, TQ, LANES), lambda b, p, i: (b, i, p)),
        out_shape=jax.ShapeDtypeStruct((batch, seq, SB_WIDTH), jnp.bfloat16),
        scratch_shapes=[
            pltpu.VMEM((2, TQ, LANES), jnp.float32),
            pltpu.VMEM((2, TQ, 1), jnp.float32),
            pltpu.VMEM((TK, TK), jnp.bfloat16),
        ],
        compiler_params=pltpu.CompilerParams(
            dimension_semantics=("arbitrary", "arbitrary", "arbitrary"), vmem_limit_bytes=VMEM_LIMIT),
        name="sb_attention",
    )(qkv, qkv, qkv, gates)


def _da_kernel(slope_ref, q_ref, k_ref, v_ref, g_ref, lq1_ref, lk1_ref, lq2_ref, lk2_ref, sg_ref, o_ref,
               acc_ref, m_ref, l_ref, *, lam_init):
    qi = pl.program_id(2)
    slope = slope_ref[pl.program_id(1)]

    lane = lax.broadcasted_iota(jnp.int32, (1, LANES), 1)
    q = q_ref[0] * jnp.asarray(HEAD_DIM ** -0.5, jnp.bfloat16)
    zero = jnp.zeros_like(q)
    q_halves = (jnp.where(lane < HEAD_DIM, q, zero), jnp.where(lane >= HEAD_DIM, q, zero))

    acc_ref[...] = jnp.zeros_like(acc_ref)
    l_ref[...] = jnp.zeros_like(l_ref)
    m_ref[...] = jnp.full_like(m_ref, NEG_BIG)

    row = lax.broadcasted_iota(jnp.int32, (TQ, TK), 0)
    col = lax.broadcasted_iota(jnp.int32, (TQ, TK), 1)

    def block(j, diagonal):
        start = pl.multiple_of(j * TK, TK)
        k = k_ref[0, pl.ds(start, TK), :]
        v = v_ref[0, pl.ds(start, TK), :]
        dist = (row - col + (qi - j) * TK).astype(jnp.float32)
        bias = -slope * jnp.abs(dist)
        if diagonal:
            bias = jnp.where((col // CHUNK) <= (row // CHUNK), bias, NEG_BIG)
        for c in range(2):
            s = _dot_nt(q_halves[c], k) + bias
            m_old = m_ref[c]
            m_new = jnp.maximum(m_old, jnp.max(s, axis=-1, keepdims=True))
            p = jnp.exp(s - m_new)
            alpha = jnp.exp(m_old - m_new)
            l_ref[c] = alpha * l_ref[c] + jnp.sum(p, axis=-1, keepdims=True)
            acc_ref[c] = alpha * acc_ref[c] + _dot(p.astype(jnp.bfloat16), v)
            m_ref[c] = m_new

    block(qi, True)

    def body(it, carry):
        block(qi - 1 - it, False)
        return carry

    lax.fori_loop(0, qi, body, 0)

    lam = (jnp.exp(jnp.sum(lq1_ref[...] * lk1_ref[...], axis=-1, keepdims=True))
           - jnp.exp(jnp.sum(lq2_ref[...] * lk2_ref[...], axis=-1, keepdims=True)) + lam_init)
    o = acc_ref[0] / l_ref[0] - lam * (acc_ref[1] / l_ref[1])
    o = o * lax.rsqrt(jnp.mean(o * o, axis=-1, keepdims=True) + RMS_EPS) * sg_ref[...]
    o = o * (1.0 - lam_init)
    g = g_ref[0]
    o_ref[0] = (o * (g * jax.nn.sigmoid(g))).astype(o_ref.dtype)


def _da_attention(slopes, qkv, gates, lq1, lk1, lq2, lk2, subln_g, batch, seq, lam_init):
    nq = seq // TQ
    q_blk = 3 * SB_WIDTH // LANES
    k_blk = q_blk + DA_HEADS
    v_blk = k_blk + DA_HEADS
    g_blk = SB_WIDTH // LANES
    vec = pl.BlockSpec((1, HEAD_DIM), lambda b, h, i: (0, 0))
    return pl.pallas_call(
        functools.partial(_da_kernel, lam_init=lam_init),
        grid=(batch, DA_HEADS, nq),
        in_specs=[
            pl.BlockSpec(memory_space=pltpu.SMEM),
            pl.BlockSpec((1, TQ, LANES), lambda b, h, i: (b, i, q_blk + h)),
            pl.BlockSpec((1, seq, LANES), lambda b, h, i: (b, 0, k_blk + h)),
            pl.BlockSpec((1, seq, LANES), lambda b, h, i: (b, 0, v_blk + h)),
            pl.BlockSpec((1, TQ, LANES), lambda b, h, i: (b, i, g_blk + h)),
            vec, vec, vec, vec,
            pl.BlockSpec((1, LANES), lambda b, h, i: (0, 0)),
        ],
        out_specs=pl.BlockSpec((1, TQ, LANES), lambda b, h, i: (b, i, h)),
        out_shape=jax.ShapeDtypeStruct((batch, seq, DA_WIDTH), jnp.bfloat16),
        scratch_shapes=[
            pltpu.VMEM((2, TQ, LANES), jnp.float32),
            pltpu.VMEM((2, TQ, 1), jnp.float32),
            pltpu.VMEM((2, TQ, 1), jnp.float32),
        ],
        compiler_params=pltpu.CompilerParams(
            dimension_semantics=("arbitrary", "arbitrary", "arbitrary"), vmem_limit_bytes=VMEM_LIMIT),
        name="da_attention",
    )(slopes, qkv, qkv, qkv, gates, lq1, lk1, lq2, lk2, subln_g)


def _out_proj_kernel(x_ref, sb_ref, da_ref, w_ref, o_ref):
    y = _dot(sb_ref[...], w_ref[0:SB_WIDTH, :]) + _dot(da_ref[...], w_ref[SB_WIDTH:, :])
    o_ref[...] = x_ref[...] + y


def _out_proj(x2d, sb_mix, da_mix, w_bf16):
    m = x2d.shape[0]
    return pl.pallas_call(
        _out_proj_kernel,
        grid=(m // TM_PROJ,),
        in_specs=[
            pl.BlockSpec((TM_PROJ, D_MODEL), lambda i: (i, 0)),
            pl.BlockSpec((TM_PROJ, SB_WIDTH), lambda i: (i, 0)),
            pl.BlockSpec((TM_PROJ, DA_WIDTH), lambda i: (i, 0)),
            pl.BlockSpec((SB_WIDTH + DA_WIDTH, D_MODEL), lambda i: (0, 0)),
        ],
        out_specs=pl.BlockSpec((TM_PROJ, D_MODEL), lambda i: (i, 0)),
        out_shape=jax.ShapeDtypeStruct((m, D_MODEL), jnp.float32),
        compiler_params=pltpu.CompilerParams(
            dimension_semantics=("arbitrary",), vmem_limit_bytes=VMEM_LIMIT),
        name="out_proj",
    )(x2d, sb_mix, da_mix, w_bf16)


def kernel(x, norm_g, w_in, w_out, q_norm_g, k_norm_g, lambda_q1, lambda_k1, lambda_q2, lambda_k2, subln_g):
    batch, seq, d_model = x.shape
    depth = norm_g.shape[0]
    assert d_model == D_MODEL and seq % TQ == 0 and (batch * seq) % TM_PROJ == 0
    slopes = jnp.asarray([2.0 ** (-8.0 * (h + 1) / DA_HEADS) for h in range(DA_HEADS)], jnp.float32)
    x2d = x.reshape(batch * seq, d_model)
    for l in range(depth):
        lam_init = 0.8 - 0.6 * math.exp(-0.3 * l)
        reps = SB_WIDTH // HEAD_DIM
        qkv, gates = _norm_proj(
            x2d, norm_g[l][None, :], w_in[l].astype(jnp.bfloat16),
            jnp.tile(q_norm_g[l], reps)[None, :], jnp.tile(k_norm_g[l], reps)[None, :])
        qkv = qkv.reshape(batch, seq, -1)
        gates = gates.reshape(batch, seq, -1)
        sb_mix = _sb_attention(qkv, gates, batch, seq)
        da_mix = _da_attention(
            slopes, qkv, gates, lambda_q1[l][None, :], lambda_k1[l][None, :],
            lambda_q2[l][None, :], lambda_k2[l][None, :], subln_g[l][None, :], batch, seq, lam_init)
        x2d = _out_proj(x2d, sb_mix.reshape(batch * seq, -1), da_mix.reshape(batch * seq, -1),
                        w_out[l].astype(jnp.bfloat16))
    return x2d.reshape(batch, seq, d_model)
```

```python
import functools
import math

import jax
import jax.numpy as jnp
from jax import lax
from jax.experimental import pallas as pl
from jax.experimental.pallas import tpu as pltpu

D_MODEL = 1024
CHUNK = 64
HEAD_DIM = 64
SB_WIDTH = 512
DA_HEADS = 4
DA_WIDTH = 512
PROJ_WIDTH = 4096
RMS_EPS = 1e-6
LANES = 128
NEG_BIG = -1e30
LOG2E = 1.4426950408889634
QK_SCALE = HEAD_DIM ** -0.5

TM_PROJ = 512
TQ = 256
TK = 256
VMEM_LIMIT = 56 * 1024 * 1024
DA_SAFE_SHIFT = 40.0

_NT = (((1,), (1,)), ((), ()))


def _dot(a, b):
    return jnp.dot(a, b, preferred_element_type=jnp.float32)


def _dot_nt(a, b):
    return lax.dot_general(a, b, _NT, preferred_element_type=jnp.float32)


def _silu(g):
    return g * jax.nn.sigmoid(g)


def _norm_proj_kernel(x_ref, g_ref, w_ref, qg_ref, kg_ref, qkv_ref, gate_ref, grp_ref):
    @pl.when(pl.program_id(0) == 0)
    def _():
        r = lax.broadcasted_iota(jnp.int32, (SB_WIDTH, SB_WIDTH), 0) // HEAD_DIM
        c = lax.broadcasted_iota(jnp.int32, (SB_WIDTH, SB_WIDTH), 1) // HEAD_DIM
        grp_ref[...] = jnp.where(r == c, 1.0, 0.0).astype(jnp.bfloat16)

    x = x_ref[...]
    ms = jnp.mean(x * x, axis=-1, keepdims=True)
    h = (x * lax.rsqrt(ms + RMS_EPS) * g_ref[...]).astype(jnp.bfloat16)

    def chunk(c):
        return _dot(h, w_ref[:, c * SB_WIDTH:(c + 1) * SB_WIDTH])

    def head_norm(y, gain):
        y2 = y * y
        hi = y2.astype(jnp.bfloat16)
        lo = (y2 - hi.astype(jnp.float32)).astype(jnp.bfloat16)
        ss = _dot(hi, grp_ref[...]) + _dot(lo, grp_ref[...])
        return y * lax.rsqrt(ss * (1.0 / HEAD_DIM) + RMS_EPS) * gain

    qkv_ref[:, 0 * SB_WIDTH:1 * SB_WIDTH] = (chunk(0) * (QK_SCALE * LOG2E)).astype(jnp.bfloat16)
    qkv_ref[:, 1 * SB_WIDTH:2 * SB_WIDTH] = chunk(1).astype(jnp.bfloat16)
    qkv_ref[:, 2 * SB_WIDTH:3 * SB_WIDTH] = chunk(2).astype(jnp.bfloat16)
    gate_ref[:, 0:SB_WIDTH] = chunk(3)
    qkv_ref[:, 3 * SB_WIDTH:4 * SB_WIDTH] = head_norm(chunk(4), qg_ref[...]).astype(jnp.bfloat16)
    qkv_ref[:, 4 * SB_WIDTH:5 * SB_WIDTH] = head_norm(chunk(5), kg_ref[...]).astype(jnp.bfloat16)
    qkv_ref[:, 5 * SB_WIDTH:6 * SB_WIDTH] = chunk(6).astype(jnp.bfloat16)
    gate_ref[:, SB_WIDTH:2 * SB_WIDTH] = chunk(7)


def _norm_proj(x2d, norm_g, w_bf16, qg_t, kg_t):
    m = x2d.shape[0]
    return pl.pallas_call(
        _norm_proj_kernel,
        grid=(m // TM_PROJ,),
        in_specs=[
            pl.BlockSpec((TM_PROJ, D_MODEL), lambda i: (i, 0)),
            pl.BlockSpec((1, D_MODEL), lambda i: (0, 0)),
            pl.BlockSpec((D_MODEL, PROJ_WIDTH), lambda i: (0, 0)),
            pl.BlockSpec((1, SB_WIDTH), lambda i: (0, 0)),
            pl.BlockSpec((1, SB_WIDTH), lambda i: (0, 0)),
        ],
        out_specs=[
            pl.BlockSpec((TM_PROJ, 6 * SB_WIDTH), lambda i: (i, 0)),
            pl.BlockSpec((TM_PROJ, 2 * SB_WIDTH), lambda i: (i, 0)),
        ],
        out_shape=[
            jax.ShapeDtypeStruct((m, 6 * SB_WIDTH), jnp.bfloat16),
            jax.ShapeDtypeStruct((m, 2 * SB_WIDTH), jnp.float32),
        ],
        scratch_shapes=[pltpu.VMEM((SB_WIDTH, SB_WIDTH), jnp.bfloat16)],
        compiler_params=pltpu.CompilerParams(
            dimension_semantics=("arbitrary",), vmem_limit_bytes=VMEM_LIMIT),
        name="norm_proj",
    )(x2d, norm_g, w_bf16, qg_t, kg_t)


def _stack_halves(q, lane):
    zero = jnp.zeros_like(q)
    return jnp.concatenate([jnp.where(lane < HEAD_DIM, q, zero), jnp.where(lane >= HEAD_DIM, q, zero)], axis=0)


def _sb_kernel(q_ref, k_ref, v_ref, g_ref, o_ref, ntri_ref, *, seq):
    @pl.when((pl.program_id(0) == 0) & (pl.program_id(1) == 0))
    def _():
        r = lax.broadcasted_iota(jnp.int32, (TK, TK), 0)
        c = lax.broadcasted_iota(jnp.int32, (TK, TK), 1)
        ntri_ref[...] = jnp.where(r > c, -1.0, 0.0).astype(jnp.bfloat16)

    lane = lax.broadcasted_iota(jnp.int32, (1, LANES), 1)
    row = lax.broadcasted_iota(jnp.int32, (2 * TQ, TK), 0) & (TQ - 1)
    col = lax.broadcasted_iota(jnp.int32, (2 * TQ, TK), 1)
    mask = col < row

    for qi in range(seq // TQ):
        rows = slice(qi * TQ, (qi + 1) * TQ)
        q2 = _stack_halves(q_ref[0, rows, :], lane)
        acc = None
        car = None
        for j in range(qi, -1, -1):
            keys = slice(j * TK, (j + 1) * TK)
            z = _dot_nt(q2, k_ref[0, keys, :])
            sp = jnp.maximum(z, 0.0) + jnp.log(1.0 + jnp.exp2(-jnp.abs(z))) * LOG2E
            if j == qi:
                sp = jnp.where(mask, sp, 0.0)
            tail = _dot(sp.astype(jnp.bfloat16), ntri_ref[...])
            arg = (z - sp) + tail
            if car is not None:
                arg = arg + car
            a = jnp.exp2(arg)
            if j == qi:
                a = jnp.where(mask, a, 0.0)
            d = _dot(a.astype(jnp.bfloat16), v_ref[0, keys, :])
            acc = d if acc is None else acc + d
            if j > 0:
                rs = jnp.sum(sp, axis=-1, keepdims=True)
                car = -rs if car is None else car - rs
        out = jnp.where(lane < HEAD_DIM, acc[0:TQ, :], acc[TQ:2 * TQ, :])
        o_ref[0, rows, :] = (out * _silu(g_ref[0, rows, :])).astype(o_ref.dtype)


def _sb_attention(qkv, gates, batch, seq):
    hp = SB_WIDTH // LANES
    return pl.pallas_call(
        functools.partial(_sb_kernel, seq=seq),
        grid=(batch, hp),
        in_specs=[
            pl.BlockSpec((1, seq, LANES), lambda b, p: (b, 0, p)),
            pl.BlockSpec((1, seq, LANES), lambda b, p: (b, 0, hp + p)),
            pl.BlockSpec((1, seq, LANES), lambda b, p: (b, 0, 2 * hp + p)),
            pl.BlockSpec((1, seq, LANES), lambda b, p: (b, 0, p)),
        ],
        out_specs=pl.BlockSpec((1, seq, LANES), lambda b, p: (b, 0, p)),
        out_shape=jax.ShapeDtypeStruct((batch, seq, SB_WIDTH), jnp.bfloat16),
        scratch_shapes=[pltpu.VMEM((TK, TK), jnp.bfloat16)],
        compiler_params=pltpu.CompilerParams(
            dimension_semantics=("arbitrary", "arbitrary"), vmem_limit_bytes=VMEM_LIMIT),
        name="sb_attention",
    )(qkv, qkv, qkv, gates)


_X_ROW, _X_Q0, _X_SHIFT, _X_COL, _X_K0 = 0, 1, 2, 3, 4


def _da_kernel(par_ref, q_ref, k_ref, v_ref, g_ref, lq1_ref, lk1_ref, lq2_ref, lk2_ref, sg_ref, o_ref,
               kx_ref, vx_ref, dc_ref, sacc_ref, m_ref, l_ref, *, lam_init, seq):
    nq = seq // TQ
    slope = par_ref[pl.program_id(1)]
    shift = par_ref[DA_HEADS]
    safe = par_ref[DA_HEADS + 1]
    lane = lax.broadcasted_iota(jnp.int32, (1, LANES), 1)
    scale = jnp.asarray(QK_SCALE, jnp.bfloat16)

    def finish(rows, o):
        lam = (jnp.exp(jnp.sum(lq1_ref[...] * lk1_ref[...], axis=-1, keepdims=True))
               - jnp.exp(jnp.sum(lq2_ref[...] * lk2_ref[...], axis=-1, keepdims=True)) + lam_init)
        d = o[0] - lam * o[1]
        d = d * lax.rsqrt(jnp.mean(d * d, axis=-1, keepdims=True) + RMS_EPS) * sg_ref[...]
        d = d * (1.0 - lam_init)
        o_ref[0, rows, :] = (d * _silu(g_ref[0, rows, :])).astype(o_ref.dtype)

    @pl.when(safe > 0.5)
    def _():
        pos = lax.broadcasted_iota(jnp.int32, (seq, 1), 0)
        colf = (pos & (TK - 1)).astype(jnp.float32)
        blkf = (pos - (pos & (TK - 1))).astype(jnp.float32)
        kx = jnp.where(lane <= _X_SHIFT, 1.0,
                       jnp.where(lane == _X_COL, slope * colf, jnp.where(lane == _X_K0, slope * blkf, 0.0)))
        kx_ref[:, 0:LANES] = k_ref[0]
        kx_ref[:, LANES:2 * LANES] = kx.astype(jnp.bfloat16)
        vx_ref[:, 0:LANES] = v_ref[0]
        vx_ref[:, LANES:2 * LANES] = jnp.ones((seq, LANES), jnp.bfloat16)
        r = lax.broadcasted_iota(jnp.int32, (2 * TQ, TK), 0) & (TQ - 1)
        c = lax.broadcasted_iota(jnp.int32, (2 * TQ, TK), 1)
        dc_ref[...] = jnp.where(c <= r, 0.0, jnp.where((c // CHUNK) == (r // CHUNK),
                                                       (-2.0 * slope) * (c - r).astype(jnp.float32), NEG_BIG))
        rowf = lax.broadcasted_iota(jnp.int32, (TQ, 1), 0).astype(jnp.float32)

        for qi in range(nq):
            rows = slice(qi * TQ, (qi + 1) * TQ)
            qx = jnp.where(lane == _X_ROW, -slope * rowf,
                           jnp.where(lane == _X_Q0, -slope * float(qi * TQ),
                                     jnp.where(lane == _X_SHIFT, -shift,
                                               jnp.where((lane == _X_COL) | (lane == _X_K0), 1.0, 0.0))))
            qx = qx.astype(jnp.bfloat16)
            qs = _stack_halves(q_ref[0, rows, :] * scale, lane)
            q2 = jnp.concatenate([qs, jnp.concatenate([qx, qx], axis=0)], axis=1)
            acc = None
            for j in range(qi, -1, -1):
                keys = slice(j * TK, (j + 1) * TK)
                s = _dot_nt(q2, kx_ref[keys, :])
                if j == qi:
                    s = s + dc_ref[...]
                p = jnp.exp(s).astype(jnp.bfloat16)
                d = _dot(p, vx_ref[keys, :])
                acc = d if acc is None else acc + d
            finish(rows, (acc[0:TQ, 0:LANES] / acc[0:TQ, LANES:2 * LANES],
                          acc[TQ:2 * TQ, 0:LANES] / acc[TQ:2 * TQ, LANES:2 * LANES]))

    @pl.when(safe <= 0.5)
    def _():
        row = lax.broadcasted_iota(jnp.int32, (TQ, TK), 0)
        col = lax.broadcasted_iota(jnp.int32, (TQ, TK), 1)

        def q_block(qi, carry):
            q0 = pl.multiple_of(qi * TQ, TQ)
            q = q_ref[0, pl.ds(q0, TQ), :] * scale
            zero = jnp.zeros_like(q)
            q_halves = (jnp.where(lane < HEAD_DIM, q, zero), jnp.where(lane >= HEAD_DIM, q, zero))
            sacc_ref[...] = jnp.zeros_like(sacc_ref)
            l_ref[...] = jnp.zeros_like(l_ref)
            m_ref[...] = jnp.full_like(m_ref, NEG_BIG)

            def block(j, diagonal):
                start = pl.multiple_of(j * TK, TK)
                k = k_ref[0, pl.ds(start, TK), :]
                v = v_ref[0, pl.ds(start, TK), :]
                dist = (row - col + (qi - j) * TK).astype(jnp.float32)
                bias = -slope * jnp.abs(dist)
                if diagonal:
                    bias = jnp.where((col // CHUNK) <= (row // CHUNK), bias, NEG_BIG)
                for c in range(2):
                    s = _dot_nt(q_halves[c], k) + bias
                    m_old = m_ref[c]
                    m_new = jnp.maximum(m_old, jnp.max(s, axis=-1, keepdims=True))
                    p = jnp.exp(s - m_new)
                    alpha = jnp.exp(m_old - m_new)
                    l_ref[c] = alpha * l_ref[c] + jnp.sum(p, axis=-1, keepdims=True)
                    sacc_ref[c] = alpha * sacc_ref[c] + _dot(p.astype(jnp.bfloat16), v)
                    m_ref[c] = m_new

            block(qi, True)

            def body(it, c):
                block(qi - 1 - it, False)
                return c

            lax.fori_loop(0, qi, body, 0)
            finish(pl.ds(q0, TQ), (sacc_ref[0] / l_ref[0], sacc_ref[1] / l_ref[1]))
            return carry

        lax.fori_loop(0, nq, q_block, 0)


def _da_attention(params, qkv, gates, lq1, lk1, lq2, lk2, subln_g, batch, seq, lam_init):
    q_blk = 3 * SB_WIDTH // LANES
    k_blk = q_blk + DA_HEADS
    v_blk = k_blk + DA_HEADS
    g_blk = SB_WIDTH // LANES
    vec = pl.BlockSpec((1, HEAD_DIM), lambda b, h: (0, 0))
    return pl.pallas_call(
        functools.partial(_da_kernel, lam_init=lam_init, seq=seq),
        grid=(batch, DA_HEADS),
        in_specs=[
            pl.BlockSpec(memory_space=pltpu.SMEM),
            pl.BlockSpec((1, seq, LANES), lambda b, h: (b, 0, q_blk + h)),
            pl.BlockSpec((1, seq, LANES), lambda b, h: (b, 0, k_blk + h)),
            pl.BlockSpec((1, seq, LANES), lambda b, h: (b, 0, v_blk + h)),
            pl.BlockSpec((1, seq, LANES), lambda b, h: (b, 0, g_blk + h)),
            vec, vec, vec, vec,
            pl.BlockSpec((1, LANES), lambda b, h: (0, 0)),
        ],
        out_specs=pl.BlockSpec((1, seq, LANES), lambda b, h: (b, 0, h)),
        out_shape=jax.ShapeDtypeStruct((batch, seq, DA_WIDTH), jnp.bfloat16),
        scratch_shapes=[
            pltpu.VMEM((seq, 2 * LANES), jnp.bfloat16),
            pltpu.VMEM((seq, 2 * LANES), jnp.bfloat16),
            pltpu.VMEM((2 * TQ, TK), jnp.float32),
            pltpu.VMEM((2, TQ, LANES), jnp.float32),
            pltpu.VMEM((2, TQ, 1), jnp.float32),
            pltpu.VMEM((2, TQ, 1), jnp.float32),
        ],
        compiler_params=pltpu.CompilerParams(
            dimension_semantics=("arbitrary", "arbitrary"), vmem_limit_bytes=VMEM_LIMIT),
        name="da_attention",
    )(params, qkv, qkv, qkv, gates, lq1, lk1, lq2, lk2, subln_g)


def _out_proj_kernel(x_ref, sb_ref, da_ref, w_ref, o_ref):
    y = _dot(sb_ref[...], w_ref[0:SB_WIDTH, :]) + _dot(da_ref[...], w_ref[SB_WIDTH:, :])
    o_ref[...] = x_ref[...] + y


def _out_proj(x2d, sb_mix, da_mix, w_bf16):
    m = x2d.shape[0]
    return pl.pallas_call(
        _out_proj_kernel,
        grid=(m // TM_PROJ,),
        in_specs=[
            pl.BlockSpec((TM_PROJ, D_MODEL), lambda i: (i, 0)),
            pl.BlockSpec((TM_PROJ, SB_WIDTH), lambda i: (i, 0)),
            pl.BlockSpec((TM_PROJ, DA_WIDTH), lambda i: (i, 0)),
            pl.BlockSpec((SB_WIDTH + DA_WIDTH, D_MODEL), lambda i: (0, 0)),
        ],
        out_specs=pl.BlockSpec((TM_PROJ, D_MODEL), lambda i: (i, 0)),
        out_shape=jax.ShapeDtypeStruct((m, D_MODEL), jnp.float32),
        compiler_params=pltpu.CompilerParams(
            dimension_semantics=("arbitrary",), vmem_limit_bytes=VMEM_LIMIT),
        name="out_proj",
    )(x2d, sb_mix, da_mix, w_bf16)


def kernel(x, norm_g, w_in, w_out, q_norm_g, k_norm_g, lambda_q1, lambda_k1, lambda_q2, lambda_k2, subln_g):
    batch, seq, d_model = x.shape
    depth = norm_g.shape[0]
    assert d_model == D_MODEL and seq % TQ == 0 and (batch * seq) % TM_PROJ == 0
    slopes = jnp.asarray([2.0 ** (-8.0 * (h + 1) / DA_HEADS) for h in range(DA_HEADS)], jnp.float32)
    x2d = x.reshape(batch * seq, d_model)
    for l in range(depth):
        lam_init = 0.8 - 0.6 * math.exp(-0.3 * l)
        reps = SB_WIDTH // HEAD_DIM
        shift = HEAD_DIM * QK_SCALE * jnp.max(jnp.abs(q_norm_g[l])) * jnp.max(jnp.abs(k_norm_g[l]))
        params = jnp.concatenate([slopes, jnp.stack([shift, (shift < DA_SAFE_SHIFT).astype(jnp.float32)])])
        qkv, gates = _norm_proj(
            x2d, norm_g[l][None, :], w_in[l].astype(jnp.bfloat16),
            jnp.tile(q_norm_g[l], reps)[None, :], jnp.tile(k_norm_g[l], reps)[None, :])
        qkv = qkv.reshape(batch, seq, -1)
        gates = gates.reshape(batch, seq, -1)
        sb_mix = _sb_attention(qkv, gates, batch, seq)
        da_mix = _da_attention(
            params, qkv, gates, lambda_q1[l][None, :], lambda_k1[l][None, :],
            lambda_q2[l][None, :], lambda_k2[l][None, :], subln_g[l][None, :], batch, seq, lam_init)
        x2d = _out_proj(x2d, sb_mix.reshape(batch * seq, -1), da_mix.reshape(batch * seq, -1),
                        w_out[l].astype(jnp.bfloat16))
    return x2d.reshape(batch, seq, d_model)
```

```python
import functools
import math

import jax
import jax.numpy as jnp
from jax import lax
from jax.experimental import pallas as pl
from jax.experimental.pallas import tpu as pltpu

D_MODEL = 1024
CHUNK = 64
HEAD_DIM = 64
SB_WIDTH = 512
DA_HEADS = 4
DA_WIDTH = 512
PROJ_WIDTH = 4096
RMS_EPS = 1e-6
LANES = 128
NEG_BIG = -1e30
LOG2E = 1.4426950408889634
QK_SCALE = HEAD_DIM ** -0.5

TM_PROJ = 512
TQ = 256
TK = 256
VMEM_LIMIT = 56 * 1024 * 1024
DA_SAFE_SHIFT = 40.0
SB_DEAD_LOG2 = -151.0

_NT = (((1,), (1,)), ((), ()))


def _dot(a, b):
    return jnp.dot(a, b, preferred_element_type=jnp.float32)


def _dot_nt(a, b):
    return lax.dot_general(a, b, _NT, preferred_element_type=jnp.float32)


def _silu(g):
    return g * jax.nn.sigmoid(g)


def _norm_proj_kernel(x_ref, g_ref, w_ref, qg_ref, kg_ref, qkv_ref, gate_ref, grp_ref):
    @pl.when(pl.program_id(0) == 0)
    def _():
        r = lax.broadcasted_iota(jnp.int32, (SB_WIDTH, SB_WIDTH), 0) // HEAD_DIM
        c = lax.broadcasted_iota(jnp.int32, (SB_WIDTH, SB_WIDTH), 1) // HEAD_DIM
        grp_ref[...] = jnp.where(r == c, 1.0, 0.0).astype(jnp.bfloat16)

    x = x_ref[...]
    ms = jnp.mean(x * x, axis=-1, keepdims=True)
    h = (x * lax.rsqrt(ms + RMS_EPS) * g_ref[...]).astype(jnp.bfloat16)

    def chunk(c):
        return _dot(h, w_ref[:, c * SB_WIDTH:(c + 1) * SB_WIDTH])

    def head_norm(y, gain):
        ss = _dot((y * y).astype(jnp.bfloat16), grp_ref[...])
        return y * lax.rsqrt(ss * (1.0 / HEAD_DIM) + RMS_EPS) * gain

    qkv_ref[:, 0 * SB_WIDTH:1 * SB_WIDTH] = (chunk(0) * (QK_SCALE * LOG2E)).astype(jnp.bfloat16)
    qkv_ref[:, 1 * SB_WIDTH:2 * SB_WIDTH] = chunk(1).astype(jnp.bfloat16)
    qkv_ref[:, 2 * SB_WIDTH:3 * SB_WIDTH] = chunk(2).astype(jnp.bfloat16)
    gate_ref[:, 0:SB_WIDTH] = chunk(3)
    qkv_ref[:, 3 * SB_WIDTH:4 * SB_WIDTH] = head_norm(chunk(4), qg_ref[...]).astype(jnp.bfloat16)
    qkv_ref[:, 4 * SB_WIDTH:5 * SB_WIDTH] = head_norm(chunk(5), kg_ref[...]).astype(jnp.bfloat16)
    qkv_ref[:, 5 * SB_WIDTH:6 * SB_WIDTH] = chunk(6).astype(jnp.bfloat16)
    gate_ref[:, SB_WIDTH:2 * SB_WIDTH] = chunk(7)


def _norm_proj(x2d, norm_g, w_bf16, qg_t, kg_t):
    m = x2d.shape[0]
    return pl.pallas_call(
        _norm_proj_kernel,
        grid=(m // TM_PROJ,),
        in_specs=[
            pl.BlockSpec((TM_PROJ, D_MODEL), lambda i: (i, 0)),
            pl.BlockSpec((1, D_MODEL), lambda i: (0, 0)),
            pl.BlockSpec((D_MODEL, PROJ_WIDTH), lambda i: (0, 0)),
            pl.BlockSpec((1, SB_WIDTH), lambda i: (0, 0)),
            pl.BlockSpec((1, SB_WIDTH), lambda i: (0, 0)),
        ],
        out_specs=[
            pl.BlockSpec((TM_PROJ, 6 * SB_WIDTH), lambda i: (i, 0)),
            pl.BlockSpec((TM_PROJ, 2 * SB_WIDTH), lambda i: (i, 0)),
        ],
        out_shape=[
            jax.ShapeDtypeStruct((m, 6 * SB_WIDTH), jnp.bfloat16),
            jax.ShapeDtypeStruct((m, 2 * SB_WIDTH), jnp.float32),
        ],
        scratch_shapes=[pltpu.VMEM((SB_WIDTH, SB_WIDTH), jnp.bfloat16)],
        compiler_params=pltpu.CompilerParams(
            dimension_semantics=("arbitrary",), vmem_limit_bytes=VMEM_LIMIT),
        name="norm_proj",
    )(x2d, norm_g, w_bf16, qg_t, kg_t)


def _stack_halves(q, lane):
    zero = jnp.zeros_like(q)
    return jnp.concatenate([jnp.where(lane < HEAD_DIM, q, zero), jnp.where(lane >= HEAD_DIM, q, zero)], axis=0)


def _sb_kernel(q_ref, k_ref, v_ref, g_ref, o_ref, ntri_ref, acc_ref, car_ref, *, seq):
    nq = seq // TQ

    @pl.when((pl.program_id(0) == 0) & (pl.program_id(1) == 0))
    def _():
        r = lax.broadcasted_iota(jnp.int32, (TK, TK), 0)
        c = lax.broadcasted_iota(jnp.int32, (TK, TK), 1)
        ntri_ref[...] = jnp.where(r > c, -1.0, 0.0).astype(jnp.bfloat16)

    lane = lax.broadcasted_iota(jnp.int32, (1, LANES), 1)
    row = lax.broadcasted_iota(jnp.int32, (2 * TQ, TK), 0) & (TQ - 1)
    col = lax.broadcasted_iota(jnp.int32, (2 * TQ, TK), 1)
    mask = col < row

    def stacked_q(qi):
        return _stack_halves(q_ref[0, qi * TQ:(qi + 1) * TQ, :], lane)

    def key_blocks(q2, first, last, acc, car, diagonal_first):
        for j in range(first, last - 1, -1):
            masked = diagonal_first and j == first
            keys = slice(j * TK, (j + 1) * TK)
            z = _dot_nt(q2, k_ref[0, keys, :])
            sp = jnp.maximum(z, 0.0) + jnp.log(1.0 + jnp.exp2(-jnp.abs(z))) * LOG2E
            if masked:
                sp = jnp.where(mask, sp, 0.0)
            tail = _dot(sp.astype(jnp.bfloat16), ntri_ref[...])
            arg = (z - sp) + tail
            if car is not None:
                arg = arg + car
            a = jnp.exp2(arg)
            if masked:
                a = jnp.where(mask, a, 0.0)
            d = _dot(a.astype(jnp.bfloat16), v_ref[0, keys, :])
            acc = d if acc is None else acc + d
            rs = jnp.sum(sp, axis=-1, keepdims=True)
            car = -rs if car is None else car - rs
        return acc, car

    for qi in range(nq):
        acc, car = key_blocks(stacked_q(qi), qi, max(qi - 1, 0), None, None, True)
        acc_ref[qi] = acc
        car_ref[qi] = car

    for qi in range(2, nq):
        @pl.when(jnp.max(car_ref[qi]) >= SB_DEAD_LOG2)
        def _(qi=qi):
            acc, _ = key_blocks(stacked_q(qi), qi - 2, 0, acc_ref[qi], car_ref[qi], False)
            acc_ref[qi] = acc

    for qi in range(nq):
        rows = slice(qi * TQ, (qi + 1) * TQ)
        out = jnp.where(lane < HEAD_DIM, acc_ref[qi, 0:TQ, :], acc_ref[qi, TQ:2 * TQ, :])
        o_ref[0, rows, :] = (out * _silu(g_ref[0, rows, :])).astype(o_ref.dtype)


def _sb_attention(qkv, gates, batch, seq):
    hp = SB_WIDTH // LANES
    return pl.pallas_call(
        functools.partial(_sb_kernel, seq=seq),
        grid=(batch, hp),
        in_specs=[
            pl.BlockSpec((1, seq, LANES), lambda b, p: (b, 0, p)),
            pl.BlockSpec((1, seq, LANES), lambda b, p: (b, 0, hp + p)),
            pl.BlockSpec((1, seq, LANES), lambda b, p: (b, 0, 2 * hp + p)),
            pl.BlockSpec((1, seq, LANES), lambda b, p: (b, 0, p)),
        ],
        out_specs=pl.BlockSpec((1, seq, LANES), lambda b, p: (b, 0, p)),
        out_shape=jax.ShapeDtypeStruct((batch, seq, SB_WIDTH), jnp.bfloat16),
        scratch_shapes=[
            pltpu.VMEM((TK, TK), jnp.bfloat16),
            pltpu.VMEM((seq // TQ, 2 * TQ, LANES), jnp.float32),
            pltpu.VMEM((seq // TQ, 2 * TQ, 1), jnp.float32),
        ],
        compiler_params=pltpu.CompilerParams(
            dimension_semantics=("arbitrary", "arbitrary"), vmem_limit_bytes=VMEM_LIMIT),
        name="sb_attention",
    )(qkv, qkv, qkv, gates)


_X_ROW, _X_Q0, _X_SHIFT, _X_COL, _X_K0 = 0, 1, 2, 3, 4


def _da_kernel(par_ref, q_ref, k_ref, v_ref, g_ref, lq1_ref, lk1_ref, lq2_ref, lk2_ref, sg_ref, o_ref,
               kx_ref, vx_ref, dc_ref, sacc_ref, m_ref, l_ref, *, lam_init, seq):
    nq = seq // TQ
    slope = par_ref[pl.program_id(1)]
    shift = par_ref[DA_HEADS]
    safe = par_ref[DA_HEADS + 1]
    lane = lax.broadcasted_iota(jnp.int32, (1, LANES), 1)
    scale = jnp.asarray(QK_SCALE, jnp.bfloat16)

    def finish(rows, o):
        lam = (jnp.exp(jnp.sum(lq1_ref[...] * lk1_ref[...], axis=-1, keepdims=True))
               - jnp.exp(jnp.sum(lq2_ref[...] * lk2_ref[...], axis=-1, keepdims=True)) + lam_init)
        d = o[0] - lam * o[1]
        d = d * lax.rsqrt(jnp.mean(d * d, axis=-1, keepdims=True) + RMS_EPS) * sg_ref[...]
        d = d * (1.0 - lam_init)
        o_ref[0, rows, :] = (d * _silu(g_ref[0, rows, :])).astype(o_ref.dtype)

    @pl.when(safe > 0.5)
    def _():
        pos = lax.broadcasted_iota(jnp.int32, (seq, 1), 0)
        colf = (pos & (TK - 1)).astype(jnp.float32)
        blkf = (pos - (pos & (TK - 1))).astype(jnp.float32)
        kx = jnp.where(lane <= _X_SHIFT, 1.0,
                       jnp.where(lane == _X_COL, slope * colf, jnp.where(lane == _X_K0, slope * blkf, 0.0)))
        kx_ref[:, 0:LANES] = k_ref[0]
        kx_ref[:, LANES:2 * LANES] = kx.astype(jnp.bfloat16)
        vx_ref[:, 0:LANES] = v_ref[0]
        vx_ref[:, LANES:2 * LANES] = jnp.ones((seq, LANES), jnp.bfloat16)
        r = lax.broadcasted_iota(jnp.int32, (2 * TQ, TK), 0) & (TQ - 1)
        c = lax.broadcasted_iota(jnp.int32, (2 * TQ, TK), 1)
        dc_ref[...] = jnp.where(c <= r, 0.0, jnp.where((c // CHUNK) == (r // CHUNK),
                                                       (-2.0 * slope) * (c - r).astype(jnp.float32), NEG_BIG))
        rowf = lax.broadcasted_iota(jnp.int32, (TQ, 1), 0).astype(jnp.float32)

        for qi in range(nq):
            rows = slice(qi * TQ, (qi + 1) * TQ)
            qx = jnp.where(lane == _X_ROW, -slope * rowf,
                           jnp.where(lane == _X_Q0, -slope * float(qi * TQ),
                                     jnp.where(lane == _X_SHIFT, -shift,
                                               jnp.where((lane == _X_COL) | (lane == _X_K0), 1.0, 0.0))))
            qx = qx.astype(jnp.bfloat16)
            qs = _stack_halves(q_ref[0, rows, :] * scale, lane)
            q2 = jnp.concatenate([qs, jnp.concatenate([qx, qx], axis=0)], axis=1)
            acc = None
            for j in range(qi, -1, -1):
                keys = slice(j * TK, (j + 1) * TK)
                s = _dot_nt(q2, kx_ref[keys, :])
                if j == qi:
                    s = s + dc_ref[...]
                p = jnp.exp(s).astype(jnp.bfloat16)
                d = _dot(p, vx_ref[keys, :])
                acc = d if acc is None else acc + d
            finish(rows, (acc[0:TQ, 0:LANES] / acc[0:TQ, LANES:2 * LANES],
                          acc[TQ:2 * TQ, 0:LANES] / acc[TQ:2 * TQ, LANES:2 * LANES]))

    @pl.when(safe <= 0.5)
    def _():
        row = lax.broadcasted_iota(jnp.int32, (TQ, TK), 0)
        col = lax.broadcasted_iota(jnp.int32, (TQ, TK), 1)

        def q_block(qi, carry):
            q0 = pl.multiple_of(qi * TQ, TQ)
            q = q_ref[0, pl.ds(q0, TQ), :] * scale
            zero = jnp.zeros_like(q)
            q_halves = (jnp.where(lane < HEAD_DIM, q, zero), jnp.where(lane >= HEAD_DIM, q, zero))
            sacc_ref[...] = jnp.zeros_like(sacc_ref)
            l_ref[...] = jnp.zeros_like(l_ref)
            m_ref[...] = jnp.full_like(m_ref, NEG_BIG)

            def block(j, diagonal):
                start = pl.multiple_of(j * TK, TK)
                k = k_ref[0, pl.ds(start, TK), :]
                v = v_ref[0, pl.ds(start, TK), :]
                dist = (row - col + (qi - j) * TK).astype(jnp.float32)
                bias = -slope * jnp.abs(dist)
                if diagonal:
                    bias = jnp.where((col // CHUNK) <= (row // CHUNK), bias, NEG_BIG)
                for c in range(2):
                    s = _dot_nt(q_halves[c], k) + bias
                    m_old = m_ref[c]
                    m_new = jnp.maximum(m_old, jnp.max(s, axis=-1, keepdims=True))
                    p = jnp.exp(s - m_new)
                    alpha = jnp.exp(m_old - m_new)
                    l_ref[c] = alpha * l_ref[c] + jnp.sum(p, axis=-1, keepdims=True)
                    sacc_ref[c] = alpha * sacc_ref[c] + _dot(p.astype(jnp.bfloat16), v)
                    m_ref[c] = m_new

            block(qi, True)

            def body(it, c):
                block(qi - 1 - it, False)
                return c

            lax.fori_loop(0, qi, body, 0)
            finish(pl.ds(q0, TQ), (sacc_ref[0] / l_ref[0], sacc_ref[1] / l_ref[1]))
            return carry

        lax.fori_loop(0, nq, q_block, 0)


def _da_attention(params, qkv, gates, lq1, lk1, lq2, lk2, subln_g, batch, seq, lam_init):
    q_blk = 3 * SB_WIDTH // LANES
    k_blk = q_blk + DA_HEADS
    v_blk = k_blk + DA_HEADS
    g_blk = SB_WIDTH // LANES
    vec = pl.BlockSpec((1, HEAD_DIM), lambda b, h: (0, 0))
    return pl.pallas_call(
        functools.partial(_da_kernel, lam_init=lam_init, seq=seq),
        grid=(batch, DA_HEADS),
        in_specs=[
            pl.BlockSpec(memory_space=pltpu.SMEM),
            pl.BlockSpec((1, seq, LANES), lambda b, h: (b, 0, q_blk + h)),
            pl.BlockSpec((1, seq, LANES), lambda b, h: (b, 0, k_blk + h)),
            pl.BlockSpec((1, seq, LANES), lambda b, h: (b, 0, v_blk + h)),
            pl.BlockSpec((1, seq, LANES), lambda b, h: (b, 0, g_blk + h)),
            vec, vec, vec, vec,
            pl.BlockSpec((1, LANES), lambda b, h: (0, 0)),
        ],
        out_specs=pl.BlockSpec((1, seq, LANES), lambda b, h: (b, 0, h)),
        out_shape=jax.ShapeDtypeStruct((batch, seq, DA_WIDTH), jnp.bfloat16),
        scratch_shapes=[
            pltpu.VMEM((seq, 2 * LANES), jnp.bfloat16),
            pltpu.VMEM((seq, 2 * LANES), jnp.bfloat16),
            pltpu.VMEM((2 * TQ, TK), jnp.float32),
            pltpu.VMEM((2, TQ, LANES), jnp.float32),
            pltpu.VMEM((2, TQ, 1), jnp.float32),
            pltpu.VMEM((2, TQ, 1), jnp.float32),
        ],
        compiler_params=pltpu.CompilerParams(
            dimension_semantics=("arbitrary", "arbitrary"), vmem_limit_bytes=VMEM_LIMIT),
        name="da_attention",
    )(params, qkv, qkv, qkv, gates, lq1, lk1, lq2, lk2, subln_g)


def _out_proj_kernel(x_ref, sb_ref, da_ref, w_ref, o_ref):
    y = _dot(sb_ref[...], w_ref[0:SB_WIDTH, :]) + _dot(da_ref[...], w_ref[SB_WIDTH:, :])
    o_ref[...] = x_ref[...] + y


def _out_proj(x2d, sb_mix, da_mix, w_bf16):
    m = x2d.shape[0]
    return pl.pallas_call(
        _out_proj_kernel,
        grid=(m // TM_PROJ,),
        in_specs=[
            pl.BlockSpec((TM_PROJ, D_MODEL), lambda i: (i, 0)),
            pl.BlockSpec((TM_PROJ, SB_WIDTH), lambda i: (i, 0)),
            pl.BlockSpec((TM_PROJ, DA_WIDTH), lambda i: (i, 0)),
            pl.BlockSpec((SB_WIDTH + DA_WIDTH, D_MODEL), lambda i: (0, 0)),
        ],
        out_specs=pl.BlockSpec((TM_PROJ, D_MODEL), lambda i: (i, 0)),
        out_shape=jax.ShapeDtypeStruct((m, D_MODEL), jnp.float32),
        compiler_params=pltpu.CompilerParams(
            dimension_semantics=("arbitrary",), vmem_limit_bytes=VMEM_LIMIT),
        name="out_proj",
    )(x2d, sb_mix, da_mix, w_bf16)


def kernel(x, norm_g, w_in, w_out, q_norm_g, k_norm_g, lambda_q1, lambda_k1, lambda_q2, lambda_k2, subln_g):
    batch, seq, d_model = x.shape
    depth = norm_g.shape[0]
    assert d_model == D_MODEL and seq % TQ == 0 and (batch * seq) % TM_PROJ == 0
    slopes = jnp.asarray([2.0 ** (-8.0 * (h + 1) / DA_HEADS) for h in range(DA_HEADS)], jnp.float32)
    x2d = x.reshape(batch * seq, d_model)
    for l in range(depth):
        lam_init = 0.8 - 0.6 * math.exp(-0.3 * l)
        reps = SB_WIDTH // HEAD_DIM
        shift = HEAD_DIM * QK_SCALE * jnp.max(jnp.abs(q_norm_g[l])) * jnp.max(jnp.abs(k_norm_g[l]))
        params = jnp.concatenate([slopes, jnp.stack([shift, (shift < DA_SAFE_SHIFT).astype(jnp.float32)])])
        qkv, gates = _norm_proj(
            x2d, norm_g[l][None, :], w_in[l].astype(jnp.bfloat16),
            jnp.tile(q_norm_g[l], reps)[None, :], jnp.tile(k_norm_g[l], reps)[None, :])
        qkv = qkv.reshape(batch, seq, -1)
        gates = gates.reshape(batch, seq, -1)
        sb_mix = _sb_attention(qkv, gates, batch, seq)
        da_mix = _da_attention(
            params, qkv, gates, lambda_q1[l][None, :], lambda_k1[l][None, :],
            lambda_q2[l][None, :], lambda_k2[l][None, :], subln_g[l][None, :], batch, seq, lam_init)
        x2d = _out_proj(x2d, sb_mix.reshape(batch * seq, -1), da_mix.reshape(batch * seq, -1),
                        w_out[l].astype(jnp.bfloat16))
    return x2d.reshape(batch, seq, d_model)
```

```python
import functools
import math

import jax
import jax.numpy as jnp
from jax import lax
from jax.experimental import pallas as pl
from jax.experimental.pallas import tpu as pltpu

D_MODEL = 1024
CHUNK = 64
HEAD_DIM = 64
SB_WIDTH = 512
DA_HEADS = 4
DA_WIDTH = 512
PROJ_WIDTH = 4096
RMS_EPS = 1e-6
LANES = 128
NEG_BIG = -1e30
LOG2E = 1.4426950408889634
QK_SCALE = HEAD_DIM ** -0.5

TM_PROJ = 512
TQ = 256
TK = 256
VMEM_LIMIT = 56 * 1024 * 1024
DA_SAFE_SHIFT = 40.0
SB_DEAD_LOG2 = -151.0
DA_DEAD_BIAS = -110.0
DA_NEAR_BLOCKS = 2

_NT = (((1,), (1,)), ((), ()))


def _dot(a, b):
    return jnp.dot(a, b, preferred_element_type=jnp.float32)


def _dot_nt(a, b):
    return lax.dot_general(a, b, _NT, preferred_element_type=jnp.float32)


def _silu(g):
    return g * jax.nn.sigmoid(g)


def _norm_proj_kernel(x_ref, g_ref, w_ref, qg_ref, kg_ref, qkv_ref, gate_ref, grp_ref):
    @pl.when(pl.program_id(0) == 0)
    def _():
        r = lax.broadcasted_iota(jnp.int32, (SB_WIDTH, SB_WIDTH), 0) // HEAD_DIM
        c = lax.broadcasted_iota(jnp.int32, (SB_WIDTH, SB_WIDTH), 1) // HEAD_DIM
        grp_ref[...] = jnp.where(r == c, 1.0, 0.0).astype(jnp.bfloat16)

    x = x_ref[...]
    ms = jnp.mean(x * x, axis=-1, keepdims=True)
    h = (x * lax.rsqrt(ms + RMS_EPS) * g_ref[...]).astype(jnp.bfloat16)

    def chunk(c):
        return _dot(h, w_ref[:, c * SB_WIDTH:(c + 1) * SB_WIDTH])

    def head_norm(y, gain):
        ss = _dot((y * y).astype(jnp.bfloat16), grp_ref[...])
        return y * lax.rsqrt(ss * (1.0 / HEAD_DIM) + RMS_EPS) * gain

    qkv_ref[:, 0 * SB_WIDTH:1 * SB_WIDTH] = (chunk(0) * (QK_SCALE * LOG2E)).astype(jnp.bfloat16)
    qkv_ref[:, 1 * SB_WIDTH:2 * SB_WIDTH] = chunk(1).astype(jnp.bfloat16)
    qkv_ref[:, 2 * SB_WIDTH:3 * SB_WIDTH] = chunk(2).astype(jnp.bfloat16)
    gate_ref[:, 0:SB_WIDTH] = chunk(3)
    qkv_ref[:, 3 * SB_WIDTH:4 * SB_WIDTH] = head_norm(chunk(4), qg_ref[...]).astype(jnp.bfloat16)
    qkv_ref[:, 4 * SB_WIDTH:5 * SB_WIDTH] = head_norm(chunk(5), kg_ref[...]).astype(jnp.bfloat16)
    qkv_ref[:, 5 * SB_WIDTH:6 * SB_WIDTH] = chunk(6).astype(jnp.bfloat16)
    gate_ref[:, SB_WIDTH:2 * SB_WIDTH] = chunk(7)


def _norm_proj(x2d, norm_g, w_bf16, qg_t, kg_t):
    m = x2d.shape[0]
    return pl.pallas_call(
        _norm_proj_kernel,
        grid=(m // TM_PROJ,),
        in_specs=[
            pl.BlockSpec((TM_PROJ, D_MODEL), lambda i: (i, 0)),
            pl.BlockSpec((1, D_MODEL), lambda i: (0, 0)),
            pl.BlockSpec((D_MODEL, PROJ_WIDTH), lambda i: (0, 0)),
            pl.BlockSpec((1, SB_WIDTH), lambda i: (0, 0)),
            pl.BlockSpec((1, SB_WIDTH), lambda i: (0, 0)),
        ],
        out_specs=[
            pl.BlockSpec((TM_PROJ, 6 * SB_WIDTH), lambda i: (i, 0)),
            pl.BlockSpec((TM_PROJ, 2 * SB_WIDTH), lambda i: (i, 0)),
        ],
        out_shape=[
            jax.ShapeDtypeStruct((m, 6 * SB_WIDTH), jnp.bfloat16),
            jax.ShapeDtypeStruct((m, 2 * SB_WIDTH), jnp.float32),
        ],
        scratch_shapes=[pltpu.VMEM((SB_WIDTH, SB_WIDTH), jnp.bfloat16)],
        compiler_params=pltpu.CompilerParams(
            dimension_semantics=("arbitrary",), vmem_limit_bytes=VMEM_LIMIT),
        name="norm_proj",
    )(x2d, norm_g, w_bf16, qg_t, kg_t)


def _stack_halves(q, lane):
    zero = jnp.zeros_like(q)
    return jnp.concatenate([jnp.where(lane < HEAD_DIM, q, zero), jnp.where(lane >= HEAD_DIM, q, zero)], axis=0)


def _sb_kernel(q_ref, k_ref, v_ref, g_ref, da_ref, x_ref, wsb_ref, wda_ref, o_ref,
               ntri_ref, acc_ref, car_ref, *, seq):
    nq = seq // TQ

    @pl.when((pl.program_id(0) == 0) & (pl.program_id(1) == 0))
    def _():
        r = lax.broadcasted_iota(jnp.int32, (TK, TK), 0)
        c = lax.broadcasted_iota(jnp.int32, (TK, TK), 1)
        ntri_ref[...] = jnp.where(r > c, -1.0, 0.0).astype(jnp.bfloat16)

    lane = lax.broadcasted_iota(jnp.int32, (1, LANES), 1)
    row = lax.broadcasted_iota(jnp.int32, (2 * TQ, TK), 0) & (TQ - 1)
    col = lax.broadcasted_iota(jnp.int32, (2 * TQ, TK), 1)
    mask = col < row

    def stacked_q(qi):
        return _stack_halves(q_ref[0, qi * TQ:(qi + 1) * TQ, :], lane)

    def key_blocks(q2, first, last, acc, car, diagonal_first):
        for j in range(first, last - 1, -1):
            masked = diagonal_first and j == first
            keys = slice(j * TK, (j + 1) * TK)
            z = _dot_nt(q2, k_ref[0, keys, :])
            sp = jnp.maximum(z, 0.0) + jnp.log(1.0 + jnp.exp2(-jnp.abs(z))) * LOG2E
            if masked:
                sp = jnp.where(mask, sp, 0.0)
            tail = _dot(sp.astype(jnp.bfloat16), ntri_ref[...])
            arg = (z - sp) + tail
            if car is not None:
                arg = arg + car
            a = jnp.exp2(arg)
            if masked:
                a = jnp.where(mask, a, 0.0)
            d = _dot(a.astype(jnp.bfloat16), v_ref[0, keys, :])
            acc = d if acc is None else acc + d
            rs = jnp.sum(sp, axis=-1, keepdims=True)
            car = -rs if car is None else car - rs
        return acc, car

    alive = {}
    for qi in range(nq):
        acc, car = key_blocks(stacked_q(qi), qi, max(qi - 1, 0), None, None, True)
        if qi > 1:
            alive[qi] = jnp.max(car) >= SB_DEAD_LOG2
            car_ref[qi] = car
        acc_ref[qi] = acc

    for qi in range(2, nq):
        @pl.when(alive[qi])
        def _(qi=qi):
            acc, _ = key_blocks(stacked_q(qi), qi - 2, 0, acc_ref[qi], car_ref[qi], False)
            acc_ref[qi] = acc

    def project(first):
        w_pair = jnp.concatenate([wsb_ref[...], wda_ref[...]], axis=0)
        for qi in range(nq):
            rows = slice(qi * TQ, (qi + 1) * TQ)
            out = jnp.where(lane < HEAD_DIM, acc_ref[qi, 0:TQ, :], acc_ref[qi, TQ:2 * TQ, :])
            sb = (out * _silu(g_ref[0, rows, :])).astype(jnp.bfloat16)
            y = _dot(jnp.concatenate([sb, da_ref[0, rows, :]], axis=1), w_pair)
            base = x_ref[0, rows, :] if first else o_ref[0, rows, :]
            o_ref[0, rows, :] = base + y

    @pl.when(pl.program_id(1) == 0)
    def _():
        project(True)

    @pl.when(pl.program_id(1) != 0)
    def _():
        project(False)


def _sb_attention_out(qkv, gates, da_mix, x, w_out_bf16, batch, seq):
    hp = SB_WIDTH // LANES
    assert DA_HEADS == hp
    return pl.pallas_call(
        functools.partial(_sb_kernel, seq=seq),
        grid=(batch, hp),
        in_specs=[
            pl.BlockSpec((1, seq, LANES), lambda b, p: (b, 0, p)),
            pl.BlockSpec((1, seq, LANES), lambda b, p: (b, 0, hp + p)),
            pl.BlockSpec((1, seq, LANES), lambda b, p: (b, 0, 2 * hp + p)),
            pl.BlockSpec((1, seq, LANES), lambda b, p: (b, 0, p)),
            pl.BlockSpec((1, seq, LANES), lambda b, p: (b, 0, p)),
            pl.BlockSpec((1, seq, D_MODEL), lambda b, p: (b, 0, 0)),
            pl.BlockSpec((LANES, D_MODEL), lambda b, p: (p, 0)),
            pl.BlockSpec((LANES, D_MODEL), lambda b, p: (hp + p, 0)),
        ],
        out_specs=pl.BlockSpec((1, seq, D_MODEL), lambda b, p: (b, 0, 0)),
        out_shape=jax.ShapeDtypeStruct((batch, seq, D_MODEL), jnp.float32),
        scratch_shapes=[
            pltpu.VMEM((TK, TK), jnp.bfloat16),
            pltpu.VMEM((seq // TQ, 2 * TQ, LANES), jnp.float32),
            pltpu.VMEM((seq // TQ, 2 * TQ, 1), jnp.float32),
        ],
        compiler_params=pltpu.CompilerParams(
            dimension_semantics=("arbitrary", "arbitrary"), vmem_limit_bytes=VMEM_LIMIT),
        name="sb_attention_out",
    )(qkv, qkv, qkv, gates, da_mix, x, w_out_bf16, w_out_bf16)


_X_ROW, _X_Q0, _X_SHIFT, _X_COL, _X_K0 = 0, 1, 2, 3, 4


def _da_kernel(par_ref, q_ref, k_ref, v_ref, g_ref, lq1_ref, lk1_ref, lq2_ref, lk2_ref, sg_ref, o_ref,
               kx_ref, vx_ref, dc_ref, sacc_ref, m_ref, l_ref, *, lam_init, seq):
    nq = seq // TQ
    slope = par_ref[pl.program_id(1)]
    shift = par_ref[DA_HEADS]
    safe = par_ref[DA_HEADS + 1]
    lane = lax.broadcasted_iota(jnp.int32, (1, LANES), 1)
    scale = jnp.asarray(QK_SCALE, jnp.bfloat16)

    def finish(rows, o):
        lam = (jnp.exp(jnp.sum(lq1_ref[...] * lk1_ref[...], axis=-1, keepdims=True))
               - jnp.exp(jnp.sum(lq2_ref[...] * lk2_ref[...], axis=-1, keepdims=True)) + lam_init)
        d = o[0] - lam * o[1]
        d = d * lax.rsqrt(jnp.mean(d * d, axis=-1, keepdims=True) + RMS_EPS) * sg_ref[...]
        d = d * (1.0 - lam_init)
        o_ref[0, rows, :] = (d * _silu(g_ref[0, rows, :])).astype(o_ref.dtype)

    def fixed_shift(skip_far):
        pos = lax.broadcasted_iota(jnp.int32, (seq, 1), 0)
        colf = (pos & (TK - 1)).astype(jnp.float32)
        blkf = (pos - (pos & (TK - 1))).astype(jnp.float32)
        kx = jnp.where(lane <= _X_SHIFT, 1.0,
                       jnp.where(lane == _X_COL, slope * colf, jnp.where(lane == _X_K0, slope * blkf, 0.0)))
        kx_ref[:, 0:LANES] = k_ref[0]
        kx_ref[:, LANES:2 * LANES] = kx.astype(jnp.bfloat16)
        vx_ref[:, 0:LANES] = v_ref[0]
        vx_ref[:, LANES:2 * LANES] = jnp.ones((seq, LANES), jnp.bfloat16)
        r = lax.broadcasted_iota(jnp.int32, (2 * TQ, TK), 0) & (TQ - 1)
        c = lax.broadcasted_iota(jnp.int32, (2 * TQ, TK), 1)
        dc_ref[...] = jnp.where(c <= r, 0.0, jnp.where((c // CHUNK) == (r // CHUNK),
                                                       (-2.0 * slope) * (c - r).astype(jnp.float32), NEG_BIG))
        rowf = lax.broadcasted_iota(jnp.int32, (TQ, 1), 0).astype(jnp.float32)

        def stacked_q(qi):
            qx = jnp.where(lane == _X_ROW, -slope * rowf,
                           jnp.where(lane == _X_Q0, -slope * float(qi * TQ),
                                     jnp.where(lane == _X_SHIFT, -shift,
                                               jnp.where((lane == _X_COL) | (lane == _X_K0), 1.0, 0.0))))
            qx = qx.astype(jnp.bfloat16)
            qs = _stack_halves(q_ref[0, qi * TQ:(qi + 1) * TQ, :] * scale, lane)
            return jnp.concatenate([qs, jnp.concatenate([qx, qx], axis=0)], axis=1)

        def key_blocks(q2, first, last, acc, diagonal_first):
            for j in range(first, last - 1, -1):
                keys = slice(j * TK, (j + 1) * TK)
                s = _dot_nt(q2, kx_ref[keys, :])
                if diagonal_first and j == first:
                    s = s + dc_ref[...]
                p = jnp.exp(s).astype(jnp.bfloat16)
                d = _dot(p, vx_ref[keys, :])
                acc = d if acc is None else acc + d
            return acc

        def normalised(acc):
            return (acc[0:TQ, 0:LANES] / acc[0:TQ, LANES:2 * LANES],
                    acc[TQ:2 * TQ, 0:LANES] / acc[TQ:2 * TQ, LANES:2 * LANES])

        for qi in range(nq):
            last = max(qi - DA_NEAR_BLOCKS, 0) if skip_far else 0
            acc = key_blocks(stacked_q(qi), qi, last, None, True)
            finish(slice(qi * TQ, (qi + 1) * TQ), normalised(acc))

    far_dead = slope * float(DA_NEAR_BLOCKS * TK + 1) > -DA_DEAD_BIAS

    @pl.when((safe > 0.5) & far_dead)
    def _():
        fixed_shift(True)

    @pl.when((safe > 0.5) & jnp.logical_not(far_dead))
    def _():
        fixed_shift(False)

    @pl.when(safe <= 0.5)
    def _():
        row = lax.broadcasted_iota(jnp.int32, (TQ, TK), 0)
        col = lax.broadcasted_iota(jnp.int32, (TQ, TK), 1)

        def q_block(qi, carry):
            q0 = pl.multiple_of(qi * TQ, TQ)
            q = q_ref[0, pl.ds(q0, TQ), :] * scale
            zero = jnp.zeros_like(q)
            q_halves = (jnp.where(lane < HEAD_DIM, q, zero), jnp.where(lane >= HEAD_DIM, q, zero))
            sacc_ref[...] = jnp.zeros_like(sacc_ref)
            l_ref[...] = jnp.zeros_like(l_ref)
            m_ref[...] = jnp.full_like(m_ref, NEG_BIG)

            def block(j, diagonal):
                start = pl.multiple_of(j * TK, TK)
                k = k_ref[0, pl.ds(start, TK), :]
                v = v_ref[0, pl.ds(start, TK), :]
                dist = (row - col + (qi - j) * TK).astype(jnp.float32)
                bias = -slope * jnp.abs(dist)
                if diagonal:
                    bias = jnp.where((col // CHUNK) <= (row // CHUNK), bias, NEG_BIG)
                for c in range(2):
                    s = _dot_nt(q_halves[c], k) + bias
                    m_old = m_ref[c]
                    m_new = jnp.maximum(m_old, jnp.max(s, axis=-1, keepdims=True))
                    p = jnp.exp(s - m_new)
                    alpha = jnp.exp(m_old - m_new)
                    l_ref[c] = alpha * l_ref[c] + jnp.sum(p, axis=-1, keepdims=True)
                    sacc_ref[c] = alpha * sacc_ref[c] + _dot(p.astype(jnp.bfloat16), v)
                    m_ref[c] = m_new

            block(qi, True)

            def body(it, c):
                block(qi - 1 - it, False)
                return c

            lax.fori_loop(0, qi, body, 0)
            finish(pl.ds(q0, TQ), (sacc_ref[0] / l_ref[0], sacc_ref[1] / l_ref[1]))
            return carry

        lax.fori_loop(0, nq, q_block, 0)


def _da_attention(params, qkv, gates, lq1, lk1, lq2, lk2, subln_g, batch, seq, lam_init):
    q_blk = 3 * SB_WIDTH // LANES
    k_blk = q_blk + DA_HEADS
    v_blk = k_blk + DA_HEADS
    g_blk = SB_WIDTH // LANES
    vec = pl.BlockSpec((1, HEAD_DIM), lambda b, h: (0, 0))
    return pl.pallas_call(
        functools.partial(_da_kernel, lam_init=lam_init, seq=seq),
        grid=(batch, DA_HEADS),
        in_specs=[
            pl.BlockSpec(memory_space=pltpu.SMEM),
            pl.BlockSpec((1, seq, LANES), lambda b, h: (b, 0, q_blk + h)),
            pl.BlockSpec((1, seq, LANES), lambda b, h: (b, 0, k_blk + h)),
            pl.BlockSpec((1, seq, LANES), lambda b, h: (b, 0, v_blk + h)),
            pl.BlockSpec((1, seq, LANES), lambda b, h: (b, 0, g_blk + h)),
            vec, vec, vec, vec,
            pl.BlockSpec((1, LANES), lambda b, h: (0, 0)),
        ],
        out_specs=pl.BlockSpec((1, seq, LANES), lambda b, h: (b, 0, h)),
        out_shape=jax.ShapeDtypeStruct((batch, seq, DA_WIDTH), jnp.bfloat16),
        scratch_shapes=[
            pltpu.VMEM((seq, 2 * LANES), jnp.bfloat16),
            pltpu.VMEM((seq, 2 * LANES), jnp.bfloat16),
            pltpu.VMEM((2 * TQ, TK), jnp.float32),
            pltpu.VMEM((2, TQ, LANES), jnp.float32),
            pltpu.VMEM((2, TQ, 1), jnp.float32),
            pltpu.VMEM((2, TQ, 1), jnp.float32),
        ],
        compiler_params=pltpu.CompilerParams(
            dimension_semantics=("arbitrary", "arbitrary"), vmem_limit_bytes=VMEM_LIMIT),
        name="da_attention",
    )(params, qkv, qkv, qkv, gates, lq1, lk1, lq2, lk2, subln_g)


def kernel(x, norm_g, w_in, w_out, q_norm_g, k_norm_g, lambda_q1, lambda_k1, lambda_q2, lambda_k2, subln_g):
    batch, seq, d_model = x.shape
    depth = norm_g.shape[0]
    assert d_model == D_MODEL and seq % TQ == 0 and (batch * seq) % TM_PROJ == 0
    slopes = jnp.asarray([2.0 ** (-8.0 * (h + 1) / DA_HEADS) for h in range(DA_HEADS)], jnp.float32)
    for l in range(depth):
        x2d = x.reshape(batch * seq, d_model)
        lam_init = 0.8 - 0.6 * math.exp(-0.3 * l)
        reps = SB_WIDTH // HEAD_DIM
        shift = HEAD_DIM * QK_SCALE * jnp.max(jnp.abs(q_norm_g[l])) * jnp.max(jnp.abs(k_norm_g[l]))
        params = jnp.concatenate([slopes, jnp.stack([shift, (shift < DA_SAFE_SHIFT).astype(jnp.float32)])])
        qkv, gates = _norm_proj(
            x2d, norm_g[l][None, :], w_in[l].astype(jnp.bfloat16),
            jnp.tile(q_norm_g[l], reps)[None, :], jnp.tile(k_norm_g[l], reps)[None, :])
        qkv = qkv.reshape(batch, seq, -1)
        gates = gates.reshape(batch, seq, -1)
        da_mix = _da_attention(
            params, qkv, gates, lambda_q1[l][None, :], lambda_k1[l][None, :],
            lambda_q2[l][None, :], lambda_k2[l][None, :], subln_g[l][None, :], batch, seq, lam_init)
        x = _sb_attention_out(qkv, gates, da_mix, x, w_out[l].astype(jnp.bfloat16), batch, seq)
    return x
```

```python
import functools
import math

import jax
import jax.numpy as jnp
from jax import lax
from jax.experimental import pallas as pl
from jax.experimental.pallas import tpu as pltpu

D_MODEL = 1024
CHUNK = 64
HEAD_DIM = 64
SB_WIDTH = 512
DA_HEADS = 4
DA_WIDTH = 512
PROJ_WIDTH = 4096
RMS_EPS = 1e-6
LANES = 128
NEG_BIG = -1e30
LOG2E = 1.4426950408889634
QK_SCALE = HEAD_DIM ** -0.5

TM_PROJ = 512
TQ = 256
TK = 256
VMEM_LIMIT = 56 * 1024 * 1024
DA_SAFE_SHIFT = 40.0
SB_DEAD_LOG2 = -151.0
DA_DEAD_BIAS = -110.0
DA_NEAR_BLOCKS = 2

_NT = (((1,), (1,)), ((), ()))


def _dot(a, b):
    return jnp.dot(a, b, preferred_element_type=jnp.float32)


def _dot_nt(a, b):
    return lax.dot_general(a, b, _NT, preferred_element_type=jnp.float32)


def _silu(g):
    return g * jax.nn.sigmoid(g)


def _norm_proj_kernel(x_ref, xn_ref, g_ref, w_ref, qg_ref, kg_ref, qkv_ref, gate_ref, grp_ref, h0_ref, h1_ref):
    i = pl.program_id(0)

    def rms_normed(x):
        ms = jnp.mean(x * x, axis=-1, keepdims=True)
        return (x * lax.rsqrt(ms + RMS_EPS) * g_ref[...]).astype(jnp.bfloat16)

    @pl.when(i == 0)
    def _():
        r = lax.broadcasted_iota(jnp.int32, (SB_WIDTH // 2, SB_WIDTH // 2), 0) // HEAD_DIM
        c = lax.broadcasted_iota(jnp.int32, (SB_WIDTH // 2, SB_WIDTH // 2), 1) // HEAD_DIM
        grp_ref[...] = jnp.where(r == c, 1.0, 0.0).astype(jnp.bfloat16)
        h0_ref[...] = rms_normed(x_ref[...])

    def project(h_ref, h_next_ref):
        h = h_ref[...]
        h_next_ref[...] = rms_normed(xn_ref[...])

        def chunk(c):
            return _dot(h, w_ref[0, :, c * SB_WIDTH:(c + 1) * SB_WIDTH])

        def head_norm(y, gain):
            y2 = (y * y).astype(jnp.bfloat16)
            half = SB_WIDTH // 2
            ss = jnp.concatenate([_dot(y2[:, 0:half], grp_ref[...]), _dot(y2[:, half:], grp_ref[...])], axis=1)
            return y * lax.rsqrt(ss * (1.0 / HEAD_DIM) + RMS_EPS) * gain

        qkv_ref[:, 0 * SB_WIDTH:1 * SB_WIDTH] = (chunk(0) * (QK_SCALE * LOG2E)).astype(jnp.bfloat16)
        qkv_ref[:, 1 * SB_WIDTH:2 * SB_WIDTH] = chunk(1).astype(jnp.bfloat16)
        qkv_ref[:, 2 * SB_WIDTH:3 * SB_WIDTH] = chunk(2).astype(jnp.bfloat16)
        gate_ref[:, 0:SB_WIDTH] = chunk(3)
        qkv_ref[:, 3 * SB_WIDTH:4 * SB_WIDTH] = head_norm(chunk(4), qg_ref[...]).astype(jnp.bfloat16)
        qkv_ref[:, 4 * SB_WIDTH:5 * SB_WIDTH] = head_norm(chunk(5), kg_ref[...]).astype(jnp.bfloat16)
        qkv_ref[:, 5 * SB_WIDTH:6 * SB_WIDTH] = chunk(6).astype(jnp.bfloat16)
        gate_ref[:, SB_WIDTH:2 * SB_WIDTH] = chunk(7)

    @pl.when(i % 2 == 0)
    def _():
        project(h0_ref, h1_ref)

    @pl.when(i % 2 == 1)
    def _():
        project(h1_ref, h0_ref)


def _norm_proj(x2d, norm_g, w_bf16, layer, qg_t, kg_t):
    m = x2d.shape[0]
    last = m // TM_PROJ - 1
    return pl.pallas_call(
        _norm_proj_kernel,
        grid=(m // TM_PROJ,),
        in_specs=[
            pl.BlockSpec((TM_PROJ, D_MODEL), lambda i: (i, 0)),
            pl.BlockSpec((TM_PROJ, D_MODEL), lambda i: (jnp.minimum(i + 1, last), 0)),
            pl.BlockSpec((1, D_MODEL), lambda i: (0, 0)),
            pl.BlockSpec((1, D_MODEL, PROJ_WIDTH), lambda i: (layer, 0, 0)),
            pl.BlockSpec((1, SB_WIDTH), lambda i: (0, 0)),
            pl.BlockSpec((1, SB_WIDTH), lambda i: (0, 0)),
        ],
        out_specs=[
            pl.BlockSpec((TM_PROJ, 6 * SB_WIDTH), lambda i: (i, 0)),
            pl.BlockSpec((TM_PROJ, 2 * SB_WIDTH), lambda i: (i, 0)),
        ],
        out_shape=[
            jax.ShapeDtypeStruct((m, 6 * SB_WIDTH), jnp.bfloat16),
            jax.ShapeDtypeStruct((m, 2 * SB_WIDTH), jnp.float32),
        ],
        scratch_shapes=[
            pltpu.VMEM((SB_WIDTH // 2, SB_WIDTH // 2), jnp.bfloat16),
            pltpu.VMEM((TM_PROJ, D_MODEL), jnp.bfloat16),
            pltpu.VMEM((TM_PROJ, D_MODEL), jnp.bfloat16),
        ],
        compiler_params=pltpu.CompilerParams(
            dimension_semantics=("arbitrary",), vmem_limit_bytes=VMEM_LIMIT),
        name="norm_proj",
    )(x2d, x2d, norm_g, w_bf16, qg_t, kg_t)


def _stack_halves(q, lane):
    zero = jnp.zeros_like(q)
    return jnp.concatenate([jnp.where(lane < HEAD_DIM, q, zero), jnp.where(lane >= HEAD_DIM, q, zero)], axis=0)


def _sb_kernel(q_ref, k_ref, v_ref, g_ref, da_ref, x_ref, wsb_ref, wda_ref, o_ref,
               ntri_ref, acc_ref, car_ref, *, seq):
    nq = seq // TQ

    @pl.when((pl.program_id(0) == 0) & (pl.program_id(1) == 0))
    def _():
        r = lax.broadcasted_iota(jnp.int32, (TK, TK), 0)
        c = lax.broadcasted_iota(jnp.int32, (TK, TK), 1)
        ntri_ref[...] = jnp.where(r > c, -1.0, 0.0).astype(jnp.bfloat16)

    lane = lax.broadcasted_iota(jnp.int32, (1, LANES), 1)
    row = lax.broadcasted_iota(jnp.int32, (2 * TQ, TK), 0) & (TQ - 1)
    col = lax.broadcasted_iota(jnp.int32, (2 * TQ, TK), 1)
    mask = col < row

    def stacked_q(qi):
        return _stack_halves(q_ref[0, qi * TQ:(qi + 1) * TQ, :], lane)

    def key_blocks(q2, first, last, acc, car, diagonal_first):
        for j in range(first, last - 1, -1):
            masked = diagonal_first and j == first
            keys = slice(j * TK, (j + 1) * TK)
            z = _dot_nt(q2, k_ref[0, keys, :])
            sp = jnp.maximum(z, 0.0) + jnp.log(1.0 + jnp.exp2(-jnp.abs(z))) * LOG2E
            if masked:
                sp = jnp.where(mask, sp, 0.0)
            tail = _dot(sp.astype(jnp.bfloat16), ntri_ref[...])
            arg = (z - sp) + tail
            if car is not None:
                arg = arg + car
            a = jnp.exp2(arg)
            if masked:
                a = jnp.where(mask, a, 0.0)
            d = _dot(a.astype(jnp.bfloat16), v_ref[0, keys, :])
            acc = d if acc is None else acc + d
            rs = jnp.sum(sp, axis=-1, keepdims=True)
            car = -rs if car is None else car - rs
        return acc, car

    alive = {}
    for qi in range(nq):
        acc, car = key_blocks(stacked_q(qi), qi, max(qi - 1, 0), None, None, True)
        if qi > 1:
            alive[qi] = jnp.max(car) >= SB_DEAD_LOG2
            car_ref[qi] = car
        acc_ref[qi] = acc

    for qi in range(2, nq):
        @pl.when(alive[qi])
        def _(qi=qi):
            acc, _ = key_blocks(stacked_q(qi), qi - 2, 0, acc_ref[qi], car_ref[qi], False)
            acc_ref[qi] = acc

    def project(first):
        w_pair = jnp.concatenate([wsb_ref[0], wda_ref[0]], axis=0)
        for qi in range(nq):
            rows = slice(qi * TQ, (qi + 1) * TQ)
            out = jnp.where(lane < HEAD_DIM, acc_ref[qi, 0:TQ, :], acc_ref[qi, TQ:2 * TQ, :])
            sb = (out * _silu(g_ref[0, rows, :])).astype(jnp.bfloat16)
            y = _dot(jnp.concatenate([sb, da_ref[0, rows, :]], axis=1), w_pair)
            base = x_ref[0, rows, :] if first else o_ref[0, rows, :]
            o_ref[0, rows, :] = base + y

    @pl.when(pl.program_id(1) == 0)
    def _():
        project(True)

    @pl.when(pl.program_id(1) != 0)
    def _():
        project(False)


def _sb_attention_out(qkv, gates, da_mix, x, w_out_bf16, layer, batch, seq):
    hp = SB_WIDTH // LANES
    assert DA_HEADS == hp
    return pl.pallas_call(
        functools.partial(_sb_kernel, seq=seq),
        grid=(batch, hp),
        in_specs=[
            pl.BlockSpec((1, seq, LANES), lambda b, p: (b, 0, p)),
            pl.BlockSpec((1, seq, LANES), lambda b, p: (b, 0, hp + p)),
            pl.BlockSpec((1, seq, LANES), lambda b, p: (b, 0, 2 * hp + p)),
            pl.BlockSpec((1, seq, LANES), lambda b, p: (b, 0, p)),
            pl.BlockSpec((1, seq, LANES), lambda b, p: (b, 0, p)),
            pl.BlockSpec((1, seq, D_MODEL), lambda b, p: (b, 0, 0)),
            pl.BlockSpec((1, LANES, D_MODEL), lambda b, p: (layer, p, 0)),
            pl.BlockSpec((1, LANES, D_MODEL), lambda b, p: (layer, hp + p, 0)),
        ],
        out_specs=pl.BlockSpec((1, seq, D_MODEL), lambda b, p: (b, 0, 0)),
        out_shape=jax.ShapeDtypeStruct((batch, seq, D_MODEL), jnp.float32),
        scratch_shapes=[
            pltpu.VMEM((TK, TK), jnp.bfloat16),
            pltpu.VMEM((seq // TQ, 2 * TQ, LANES), jnp.float32),
            pltpu.VMEM((seq // TQ, 2 * TQ, 1), jnp.float32),
        ],
        compiler_params=pltpu.CompilerParams(
            dimension_semantics=("arbitrary", "arbitrary"), vmem_limit_bytes=VMEM_LIMIT),
        name="sb_attention_out",
    )(qkv, qkv, qkv, gates, da_mix, x, w_out_bf16, w_out_bf16)


_X_ROW, _X_Q0, _X_SHIFT, _X_COL, _X_K0 = 0, 1, 2, 3, 4


def _da_kernel(par_ref, q_ref, k_ref, v_ref, g_ref, lq1_ref, lk1_ref, lq2_ref, lk2_ref, sg_ref, o_ref,
               kx_ref, vx_ref, dc_ref, sacc_ref, m_ref, l_ref, *, lam_init, seq):
    nq = seq // TQ
    slope = par_ref[pl.program_id(1)]
    shift = par_ref[DA_HEADS]
    safe = par_ref[DA_HEADS + 1]
    lane = lax.broadcasted_iota(jnp.int32, (1, LANES), 1)
    scale = jnp.asarray(QK_SCALE, jnp.bfloat16)

    def finish(rows, o):
        lam = (jnp.exp(jnp.sum(lq1_ref[...] * lk1_ref[...], axis=-1, keepdims=True))
               - jnp.exp(jnp.sum(lq2_ref[...] * lk2_ref[...], axis=-1, keepdims=True)) + lam_init)
        d = o[0] - lam * o[1]
        d = d * lax.rsqrt(jnp.mean(d * d, axis=-1, keepdims=True) + RMS_EPS) * sg_ref[...]
        d = d * (1.0 - lam_init)
        o_ref[0, rows, :] = (d * _silu(g_ref[0, rows, :])).astype(o_ref.dtype)

    def fixed_shift(skip_far):
        pos = lax.broadcasted_iota(jnp.int32, (seq, 1), 0)
        colf = (pos & (TK - 1)).astype(jnp.float32)
        blkf = (pos - (pos & (TK - 1))).astype(jnp.float32)
        kx = jnp.where(lane <= _X_SHIFT, 1.0,
                       jnp.where(lane == _X_COL, slope * colf, jnp.where(lane == _X_K0, slope * blkf, 0.0)))
        kx_ref[:, 0:LANES] = k_ref[0]
        kx_ref[:, LANES:2 * LANES] = kx.astype(jnp.bfloat16)
        vx_ref[:, 0:LANES] = v_ref[0]
        vx_ref[:, LANES:2 * LANES] = jnp.ones((seq, LANES), jnp.bfloat16)
        r = lax.broadcasted_iota(jnp.int32, (2 * TQ, TK), 0) & (TQ - 1)
        c = lax.broadcasted_iota(jnp.int32, (2 * TQ, TK), 1)
        dc_ref[...] = jnp.where(c <= r, 0.0, jnp.where((c // CHUNK) == (r // CHUNK),
                                                       (-2.0 * slope) * (c - r).astype(jnp.float32), NEG_BIG))
        rowf = lax.broadcasted_iota(jnp.int32, (TQ, 1), 0).astype(jnp.float32)

        def stacked_q(qi):
            qx = jnp.where(lane == _X_ROW, -slope * rowf,
                           jnp.where(lane == _X_Q0, -slope * float(qi * TQ),
                                     jnp.where(lane == _X_SHIFT, -shift,
                                               jnp.where((lane == _X_COL) | (lane == _X_K0), 1.0, 0.0))))
            qx = qx.astype(jnp.bfloat16)
            qs = _stack_halves(q_ref[0, qi * TQ:(qi + 1) * TQ, :] * scale, lane)
            return jnp.concatenate([qs, jnp.concatenate([qx, qx], axis=0)], axis=1)

        def key_blocks(q2, first, last, acc, diagonal_first):
            for j in range(first, last - 1, -1):
                keys = slice(j * TK, (j + 1) * TK)
                s = _dot_nt(q2, kx_ref[keys, :])
                if diagonal_first and j == first:
                    s = s + dc_ref[...]
                p = jnp.exp(s).astype(jnp.bfloat16)
                d = _dot(p, vx_ref[keys, :])
                acc = d if acc is None else acc + d
            return acc

        def normalised(acc):
            return (acc[0:TQ, 0:LANES] / acc[0:TQ, LANES:2 * LANES],
                    acc[TQ:2 * TQ, 0:LANES] / acc[TQ:2 * TQ, LANES:2 * LANES])

        for qi in range(nq):
            last = max(qi - DA_NEAR_BLOCKS, 0) if skip_far else 0
            acc = key_blocks(stacked_q(qi), qi, last, None, True)
            finish(slice(qi * TQ, (qi + 1) * TQ), normalised(acc))

    far_dead = slope * float(DA_NEAR_BLOCKS * TK + 1) > -DA_DEAD_BIAS

    @pl.when((safe > 0.5) & far_dead)
    def _():
        fixed_shift(True)

    @pl.when((safe > 0.5) & jnp.logical_not(far_dead))
    def _():
        fixed_shift(False)

    @pl.when(safe <= 0.5)
    def _():
        row = lax.broadcasted_iota(jnp.int32, (TQ, TK), 0)
        col = lax.broadcasted_iota(jnp.int32, (TQ, TK), 1)

        def q_block(qi, carry):
            q0 = pl.multiple_of(qi * TQ, TQ)
            q = q_ref[0, pl.ds(q0, TQ), :] * scale
            zero = jnp.zeros_like(q)
            q_halves = (jnp.where(lane < HEAD_DIM, q, zero), jnp.where(lane >= HEAD_DIM, q, zero))
            sacc_ref[...] = jnp.zeros_like(sacc_ref)
            l_ref[...] = jnp.zeros_like(l_ref)
            m_ref[...] = jnp.full_like(m_ref, NEG_BIG)

            def block(j, diagonal):
                start = pl.multiple_of(j * TK, TK)
                k = k_ref[0, pl.ds(start, TK), :]
                v = v_ref[0, pl.ds(start, TK), :]
                dist = (row - col + (qi - j) * TK).astype(jnp.float32)
                bias = -slope * jnp.abs(dist)
                if diagonal:
                    bias = jnp.where((col // CHUNK) <= (row // CHUNK), bias, NEG_BIG)
                for c in range(2):
                    s = _dot_nt(q_halves[c], k) + bias
                    m_old = m_ref[c]
                    m_new = jnp.maximum(m_old, jnp.max(s, axis=-1, keepdims=True))
                    p = jnp.exp(s - m_new)
                    alpha = jnp.exp(m_old - m_new)
                    l_ref[c] = alpha * l_ref[c] + jnp.sum(p, axis=-1, keepdims=True)
                    sacc_ref[c] = alpha * sacc_ref[c] + _dot(p.astype(jnp.bfloat16), v)
                    m_ref[c] = m_new

            block(qi, True)

            def body(it, c):
                block(qi - 1 - it, False)
                return c

            lax.fori_loop(0, qi, body, 0)
            finish(pl.ds(q0, TQ), (sacc_ref[0] / l_ref[0], sacc_ref[1] / l_ref[1]))
            return carry

        lax.fori_loop(0, nq, q_block, 0)


def _da_attention(params, qkv, gates, lq1, lk1, lq2, lk2, subln_g, batch, seq, lam_init):
    q_blk = 3 * SB_WIDTH // LANES
    k_blk = q_blk + DA_HEADS
    v_blk = k_blk + DA_HEADS
    g_blk = SB_WIDTH // LANES
    vec = pl.BlockSpec((1, HEAD_DIM), lambda b, h: (0, 0))
    return pl.pallas_call(
        functools.partial(_da_kernel, lam_init=lam_init, seq=seq),
        grid=(batch, DA_HEADS),
        in_specs=[
            pl.BlockSpec(memory_space=pltpu.SMEM),
            pl.BlockSpec((1, seq, LANES), lambda b, h: (b, 0, q_blk + h)),
            pl.BlockSpec((1, seq, LANES), lambda b, h: (b, 0, k_blk + h)),
            pl.BlockSpec((1, seq, LANES), lambda b, h: (b, 0, v_blk + h)),
            pl.BlockSpec((1, seq, LANES), lambda b, h: (b, 0, g_blk + h)),
            vec, vec, vec, vec,
            pl.BlockSpec((1, LANES), lambda b, h: (0, 0)),
        ],
        out_specs=pl.BlockSpec((1, seq, LANES), lambda b, h: (b, 0, h)),
        out_shape=jax.ShapeDtypeStruct((batch, seq, DA_WIDTH), jnp.bfloat16),
        scratch_shapes=[
            pltpu.VMEM((seq, 2 * LANES), jnp.bfloat16),
            pltpu.VMEM((seq, 2 * LANES), jnp.bfloat16),
            pltpu.VMEM((2 * TQ, TK), jnp.float32),
            pltpu.VMEM((2, TQ, LANES), jnp.float32),
            pltpu.VMEM((2, TQ, 1), jnp.float32),
            pltpu.VMEM((2, TQ, 1), jnp.float32),
        ],
        compiler_params=pltpu.CompilerParams(
            dimension_semantics=("arbitrary", "arbitrary"), vmem_limit_bytes=VMEM_LIMIT),
        name="da_attention",
    )(params, qkv, qkv, qkv, gates, lq1, lk1, lq2, lk2, subln_g)


def kernel(x, norm_g, w_in, w_out, q_norm_g, k_norm_g, lambda_q1, lambda_k1, lambda_q2, lambda_k2, subln_g):
    batch, seq, d_model = x.shape
    depth = norm_g.shape[0]
    assert d_model == D_MODEL and seq % TQ == 0 and (batch * seq) % TM_PROJ == 0
    slopes = jnp.asarray([2.0 ** (-8.0 * (h + 1) / DA_HEADS) for h in range(DA_HEADS)], jnp.float32)
    w_in_bf16 = w_in.astype(jnp.bfloat16)
    w_out_bf16 = w_out.astype(jnp.bfloat16)
    for l in range(depth):
        x2d = x.reshape(batch * seq, d_model)
        lam_init = 0.8 - 0.6 * math.exp(-0.3 * l)
        reps = SB_WIDTH // HEAD_DIM
        shift = HEAD_DIM * QK_SCALE * jnp.max(jnp.abs(q_norm_g[l])) * jnp.max(jnp.abs(k_norm_g[l]))
        params = jnp.concatenate([slopes, jnp.stack([shift, (shift < DA_SAFE_SHIFT).astype(jnp.float32)])])
        qkv, gates = _norm_proj(
            x2d, norm_g[l][None, :], w_in_bf16, l,
            jnp.tile(q_norm_g[l], reps)[None, :], jnp.tile(k_norm_g[l], reps)[None, :])
        qkv = qkv.reshape(batch, seq, -1)
        gates = gates.reshape(batch, seq, -1)
        da_mix = _da_attention(
            params, qkv, gates, lambda_q1[l][None, :], lambda_k1[l][None, :],
            lambda_q2[l][None, :], lambda_k2[l][None, :], subln_g[l][None, :], batch, seq, lam_init)
        x = _sb_attention_out(qkv, gates, da_mix, x, w_out_bf16, l, batch, seq)
    return x
```

```python
import functools
import math

import jax
import jax.numpy as jnp
from jax import lax
from jax.experimental import pallas as pl
from jax.experimental.pallas import tpu as pltpu

D_MODEL = 1024
CHUNK = 64
HEAD_DIM = 64
SB_WIDTH = 512
DA_HEADS = 4
DA_WIDTH = 512
PROJ_WIDTH = 4096
RMS_EPS = 1e-6
LANES = 128
NEG_BIG = -1e30
LOG2E = 1.4426950408889634
QK_SCALE = HEAD_DIM ** -0.5

TM_PROJ = 512
TQ = 256
TK = 256
VMEM_LIMIT = 56 * 1024 * 1024
DA_SAFE_SHIFT = 40.0
SB_DEAD_LOG2 = -151.0
DA_DEAD_BIAS = -110.0
DA_NEAR_BLOCKS = 2
DA_BATCH = 2

_NT = (((1,), (1,)), ((), ()))


def _dot(a, b):
    return jnp.dot(a, b, preferred_element_type=jnp.float32)


def _dot_nt(a, b):
    return lax.dot_general(a, b, _NT, preferred_element_type=jnp.float32)


def _silu(g):
    return g * jax.nn.sigmoid(g)


def _norm_proj_kernel(x_ref, xn_ref, g_ref, w_ref, qg_ref, kg_ref, qkv_ref, gate_ref, grp_ref, h0_ref, h1_ref):
    i = pl.program_id(0)

    def rms_normed(x):
        ms = jnp.mean(x * x, axis=-1, keepdims=True)
        return (x * lax.rsqrt(ms + RMS_EPS) * g_ref[...]).astype(jnp.bfloat16)

    @pl.when(i == 0)
    def _():
        r = lax.broadcasted_iota(jnp.int32, (SB_WIDTH // 2, SB_WIDTH // 2), 0) // HEAD_DIM
        c = lax.broadcasted_iota(jnp.int32, (SB_WIDTH // 2, SB_WIDTH // 2), 1) // HEAD_DIM
        grp_ref[...] = jnp.where(r == c, 1.0, 0.0).astype(jnp.bfloat16)
        h0_ref[...] = rms_normed(x_ref[...])

    def project(h_ref, h_next_ref):
        h = h_ref[...]
        h_next_ref[...] = rms_normed(xn_ref[...])

        def chunk(c):
            return _dot(h, w_ref[0, :, c * SB_WIDTH:(c + 1) * SB_WIDTH])

        def head_norm(y, gain):
            y2 = (y * y).astype(jnp.bfloat16)
            half = SB_WIDTH // 2
            ss = jnp.concatenate([_dot(y2[:, 0:half], grp_ref[...]), _dot(y2[:, half:], grp_ref[...])], axis=1)
            return y * lax.rsqrt(ss * (1.0 / HEAD_DIM) + RMS_EPS) * gain

        qkv_ref[:, 0 * SB_WIDTH:1 * SB_WIDTH] = (chunk(0) * (QK_SCALE * LOG2E)).astype(jnp.bfloat16)
        qkv_ref[:, 1 * SB_WIDTH:2 * SB_WIDTH] = chunk(1).astype(jnp.bfloat16)
        qkv_ref[:, 2 * SB_WIDTH:3 * SB_WIDTH] = chunk(2).astype(jnp.bfloat16)
        gate_ref[:, 0:SB_WIDTH] = chunk(3)
        qkv_ref[:, 3 * SB_WIDTH:4 * SB_WIDTH] = head_norm(chunk(4), qg_ref[...]).astype(jnp.bfloat16)
        qkv_ref[:, 4 * SB_WIDTH:5 * SB_WIDTH] = head_norm(chunk(5), kg_ref[...]).astype(jnp.bfloat16)
        qkv_ref[:, 5 * SB_WIDTH:6 * SB_WIDTH] = chunk(6).astype(jnp.bfloat16)
        gate_ref[:, SB_WIDTH:2 * SB_WIDTH] = chunk(7)

    @pl.when(i % 2 == 0)
    def _():
        project(h0_ref, h1_ref)

    @pl.when(i % 2 == 1)
    def _():
        project(h1_ref, h0_ref)


def _norm_proj(x2d, norm_g, w_bf16, layer, qg_t, kg_t):
    m = x2d.shape[0]
    last = m // TM_PROJ - 1
    return pl.pallas_call(
        _norm_proj_kernel,
        grid=(m // TM_PROJ,),
        in_specs=[
            pl.BlockSpec((TM_PROJ, D_MODEL), lambda i: (i, 0)),
            pl.BlockSpec((TM_PROJ, D_MODEL), lambda i: (jnp.minimum(i + 1, last), 0)),
            pl.BlockSpec((1, D_MODEL), lambda i: (0, 0)),
            pl.BlockSpec((1, D_MODEL, PROJ_WIDTH), lambda i: (layer, 0, 0)),
            pl.BlockSpec((1, SB_WIDTH), lambda i: (0, 0)),
            pl.BlockSpec((1, SB_WIDTH), lambda i: (0, 0)),
        ],
        out_specs=[
            pl.BlockSpec((TM_PROJ, 6 * SB_WIDTH), lambda i: (i, 0)),
            pl.BlockSpec((TM_PROJ, 2 * SB_WIDTH), lambda i: (i, 0)),
        ],
        out_shape=[
            jax.ShapeDtypeStruct((m, 6 * SB_WIDTH), jnp.bfloat16),
            jax.ShapeDtypeStruct((m, 2 * SB_WIDTH), jnp.float32),
        ],
        scratch_shapes=[
            pltpu.VMEM((SB_WIDTH // 2, SB_WIDTH // 2), jnp.bfloat16),
            pltpu.VMEM((TM_PROJ, D_MODEL), jnp.bfloat16),
            pltpu.VMEM((TM_PROJ, D_MODEL), jnp.bfloat16),
        ],
        compiler_params=pltpu.CompilerParams(
            dimension_semantics=("arbitrary",), vmem_limit_bytes=VMEM_LIMIT),
        name="norm_proj",
    )(x2d, x2d, norm_g, w_bf16, qg_t, kg_t)


def _stack_halves(q, lane):
    zero = jnp.zeros_like(q)
    return jnp.concatenate([jnp.where(lane < HEAD_DIM, q, zero), jnp.where(lane >= HEAD_DIM, q, zero)], axis=0)


def _sb_kernel(q_ref, k_ref, v_ref, g_ref, da_ref, x_ref, wsb_ref, wda_ref, o_ref,
               ntri_ref, acc_ref, car_ref, *, seq):
    nq = seq // TQ

    @pl.when((pl.program_id(0) == 0) & (pl.program_id(1) == 0))
    def _():
        r = lax.broadcasted_iota(jnp.int32, (TK, TK), 0)
        c = lax.broadcasted_iota(jnp.int32, (TK, TK), 1)
        ntri_ref[...] = jnp.where(r > c, -1.0, 0.0).astype(jnp.bfloat16)

    lane = lax.broadcasted_iota(jnp.int32, (1, LANES), 1)
    row = lax.broadcasted_iota(jnp.int32, (2 * TQ, TK), 0) & (TQ - 1)
    col = lax.broadcasted_iota(jnp.int32, (2 * TQ, TK), 1)
    mask = col < row

    def stacked_q(qi):
        return _stack_halves(q_ref[0, qi * TQ:(qi + 1) * TQ, :], lane)

    def key_blocks(q2, first, last, acc, car, diagonal_first):
        for j in range(first, last - 1, -1):
            masked = diagonal_first and j == first
            keys = slice(j * TK, (j + 1) * TK)
            z = _dot_nt(q2, k_ref[0, keys, :])
            sp = jnp.maximum(z, 0.0) + jnp.log(1.0 + jnp.exp2(-jnp.abs(z))) * LOG2E
            if masked:
                sp = jnp.where(mask, sp, 0.0)
            tail = _dot(sp.astype(jnp.bfloat16), ntri_ref[...])
            arg = (z - sp) + tail
            if car is not None:
                arg = arg + car
            a = jnp.exp2(arg)
            if masked:
                a = jnp.where(mask, a, 0.0)
            d = _dot(a.astype(jnp.bfloat16), v_ref[0, keys, :])
            acc = d if acc is None else acc + d
            rs = jnp.sum(sp, axis=-1, keepdims=True)
            car = -rs if car is None else car - rs
        return acc, car

    alive = {}
    for qi in range(nq):
        acc, car = key_blocks(stacked_q(qi), qi, max(qi - 1, 0), None, None, True)
        if qi > 1:
            alive[qi] = jnp.max(car) >= SB_DEAD_LOG2
            car_ref[qi] = car
        acc_ref[qi] = acc

    for qi in range(2, nq):
        @pl.when(alive[qi])
        def _(qi=qi):
            acc, _ = key_blocks(stacked_q(qi), qi - 2, 0, acc_ref[qi], car_ref[qi], False)
            acc_ref[qi] = acc

    def project(first):
        w_pair = jnp.concatenate([wsb_ref[0], wda_ref[0]], axis=0)
        for qi in range(nq):
            rows = slice(qi * TQ, (qi + 1) * TQ)
            out = jnp.where(lane < HEAD_DIM, acc_ref[qi, 0:TQ, :], acc_ref[qi, TQ:2 * TQ, :])
            sb = (out * _silu(g_ref[0, rows, :])).astype(jnp.bfloat16)
            y = _dot(jnp.concatenate([sb, da_ref[0, rows, :]], axis=1), w_pair)
            base = x_ref[0, rows, :] if first else o_ref[0, rows, :]
            o_ref[0, rows, :] = base + y

    @pl.when(pl.program_id(1) == 0)
    def _():
        project(True)

    @pl.when(pl.program_id(1) != 0)
    def _():
        project(False)


def _sb_attention_out(qkv, gates, da_mix, x, w_out_bf16, layer, batch, seq):
    hp = SB_WIDTH // LANES
    assert DA_HEADS == hp
    return pl.pallas_call(
        functools.partial(_sb_kernel, seq=seq),
        grid=(batch, hp),
        in_specs=[
            pl.BlockSpec((1, seq, LANES), lambda b, p: (b, 0, p)),
            pl.BlockSpec((1, seq, LANES), lambda b, p: (b, 0, hp + p)),
            pl.BlockSpec((1, seq, LANES), lambda b, p: (b, 0, 2 * hp + p)),
            pl.BlockSpec((1, seq, LANES), lambda b, p: (b, 0, p)),
            pl.BlockSpec((1, seq, LANES), lambda b, p: (b, 0, p)),
            pl.BlockSpec((1, seq, D_MODEL), lambda b, p: (b, 0, 0)),
            pl.BlockSpec((1, LANES, D_MODEL), lambda b, p: (layer, p, 0)),
            pl.BlockSpec((1, LANES, D_MODEL), lambda b, p: (layer, hp + p, 0)),
        ],
        out_specs=pl.BlockSpec((1, seq, D_MODEL), lambda b, p: (b, 0, 0)),
        out_shape=jax.ShapeDtypeStruct((batch, seq, D_MODEL), jnp.float32),
        scratch_shapes=[
            pltpu.VMEM((TK, TK), jnp.bfloat16),
            pltpu.VMEM((seq // TQ, 2 * TQ, LANES), jnp.float32),
            pltpu.VMEM((seq // TQ, 2 * TQ, 1), jnp.float32),
        ],
        compiler_params=pltpu.CompilerParams(
            dimension_semantics=("arbitrary", "arbitrary"), vmem_limit_bytes=VMEM_LIMIT),
        name="sb_attention_out",
    )(qkv, qkv, qkv, gates, da_mix, x, w_out_bf16, w_out_bf16)


_X_ROW, _X_Q0, _X_SHIFT, _X_COL, _X_K0 = 0, 1, 2, 3, 4


def _da_kernel(par_ref, q_ref, k_ref, v_ref, g_ref, lq1_ref, lk1_ref, lq2_ref, lk2_ref, sg_ref, o_ref,
               kx_ref, vx_ref, dc_ref, sacc_ref, m_ref, l_ref, *, lam_init, seq):
    nq = seq // TQ
    slope = par_ref[pl.program_id(1)]
    shift = par_ref[DA_HEADS]
    safe = par_ref[DA_HEADS + 1]
    lane = lax.broadcasted_iota(jnp.int32, (1, LANES), 1)
    scale = jnp.asarray(QK_SCALE, jnp.bfloat16)

    def finish(bb, rows, o):
        lam = (jnp.exp(jnp.sum(lq1_ref[...] * lk1_ref[...], axis=-1, keepdims=True))
               - jnp.exp(jnp.sum(lq2_ref[...] * lk2_ref[...], axis=-1, keepdims=True)) + lam_init)
        d = o[0] - lam * o[1]
        d = d * lax.rsqrt(jnp.mean(d * d, axis=-1, keepdims=True) + RMS_EPS) * sg_ref[...]
        d = d * (1.0 - lam_init)
        o_ref[bb, rows, :] = (d * _silu(g_ref[bb, rows, :])).astype(o_ref.dtype)

    def fixed_shift(skip_far):
        pos = lax.broadcasted_iota(jnp.int32, (seq, 1), 0)
        colf = (pos & (TK - 1)).astype(jnp.float32)
        blkf = (pos - (pos & (TK - 1))).astype(jnp.float32)
        kx = jnp.where(lane <= _X_SHIFT, 1.0,
                       jnp.where(lane == _X_COL, slope * colf, jnp.where(lane == _X_K0, slope * blkf, 0.0)))
        for bb in range(DA_BATCH):
            kx_ref[bb, :, 0:LANES] = k_ref[bb]
            kx_ref[bb, :, LANES:2 * LANES] = kx.astype(jnp.bfloat16)
            vx_ref[bb, :, 0:LANES] = v_ref[bb]
            vx_ref[bb, :, LANES:2 * LANES] = jnp.ones((seq, LANES), jnp.bfloat16)
        r = lax.broadcasted_iota(jnp.int32, (2 * TQ, TK), 0) & (TQ - 1)
        c = lax.broadcasted_iota(jnp.int32, (2 * TQ, TK), 1)
        dc_ref[...] = jnp.where(c <= r, 0.0, jnp.where((c // CHUNK) == (r // CHUNK),
                                                       (-2.0 * slope) * (c - r).astype(jnp.float32), NEG_BIG))
        rowf = lax.broadcasted_iota(jnp.int32, (TQ, 1), 0).astype(jnp.float32)

        def stacked_q(bb, qi):
            qx = jnp.where(lane == _X_ROW, -slope * rowf,
                           jnp.where(lane == _X_Q0, -slope * float(qi * TQ),
                                     jnp.where(lane == _X_SHIFT, -shift,
                                               jnp.where((lane == _X_COL) | (lane == _X_K0), 1.0, 0.0))))
            qx = qx.astype(jnp.bfloat16)
            qs = _stack_halves(q_ref[bb, qi * TQ:(qi + 1) * TQ, :] * scale, lane)
            return jnp.concatenate([qs, jnp.concatenate([qx, qx], axis=0)], axis=1)

        def key_blocks(bb, q2, first, last, acc, diagonal_first):
            for j in range(first, last - 1, -1):
                keys = slice(j * TK, (j + 1) * TK)
                s = _dot_nt(q2, kx_ref[bb, keys, :])
                if diagonal_first and j == first:
                    s = s + dc_ref[...]
                p = jnp.exp(s).astype(jnp.bfloat16)
                d = _dot(p, vx_ref[bb, keys, :])
                acc = d if acc is None else acc + d
            return acc

        def normalised(acc):
            return (acc[0:TQ, 0:LANES] / acc[0:TQ, LANES:2 * LANES],
                    acc[TQ:2 * TQ, 0:LANES] / acc[TQ:2 * TQ, LANES:2 * LANES])

        for bb in range(DA_BATCH):
            for qi in range(nq):
                last = max(qi - DA_NEAR_BLOCKS, 0) if skip_far else 0
                acc = key_blocks(bb, stacked_q(bb, qi), qi, last, None, True)
                finish(bb, slice(qi * TQ, (qi + 1) * TQ), normalised(acc))

    far_dead = slope * float(DA_NEAR_BLOCKS * TK + 1) > -DA_DEAD_BIAS

    @pl.when((safe > 0.5) & far_dead)
    def _():
        fixed_shift(True)

    @pl.when((safe > 0.5) & jnp.logical_not(far_dead))
    def _():
        fixed_shift(False)

    @pl.when(safe <= 0.5)
    def _():
        row = lax.broadcasted_iota(jnp.int32, (TQ, TK), 0)
        col = lax.broadcasted_iota(jnp.int32, (TQ, TK), 1)

        def q_block(idx, carry):
            bb = idx // nq
            qi = idx - bb * nq
            q0 = pl.multiple_of(qi * TQ, TQ)
            q = q_ref[bb, pl.ds(q0, TQ), :] * scale
            zero = jnp.zeros_like(q)
            q_halves = (jnp.where(lane < HEAD_DIM, q, zero), jnp.where(lane >= HEAD_DIM, q, zero))
            sacc_ref[...] = jnp.zeros_like(sacc_ref)
            l_ref[...] = jnp.zeros_like(l_ref)
            m_ref[...] = jnp.full_like(m_ref, NEG_BIG)

            def block(j, diagonal):
                start = pl.multiple_of(j * TK, TK)
                k = k_ref[bb, pl.ds(start, TK), :]
                v = v_ref[bb, pl.ds(start, TK), :]
                dist = (row - col + (qi - j) * TK).astype(jnp.float32)
                bias = -slope * jnp.abs(dist)
                if diagonal:
                    bias = jnp.where((col // CHUNK) <= (row // CHUNK), bias, NEG_BIG)
                for c in range(2):
                    s = _dot_nt(q_halves[c], k) + bias
                    m_old = m_ref[c]
                    m_new = jnp.maximum(m_old, jnp.max(s, axis=-1, keepdims=True))
                    p = jnp.exp(s - m_new)
                    alpha = jnp.exp(m_old - m_new)
                    l_ref[c] = alpha * l_ref[c] + jnp.sum(p, axis=-1, keepdims=True)
                    sacc_ref[c] = alpha * sacc_ref[c] + _dot(p.astype(jnp.bfloat16), v)
                    m_ref[c] = m_new

            block(qi, True)

            def body(it, c):
                block(qi - 1 - it, False)
                return c

            lax.fori_loop(0, qi, body, 0)
            finish(bb, pl.ds(q0, TQ), (sacc_ref[0] / l_ref[0], sacc_ref[1] / l_ref[1]))
            return carry

        lax.fori_loop(0, DA_BATCH * nq, q_block, 0)


def _da_attention(params, qkv, gates, lq1, lk1, lq2, lk2, subln_g, batch, seq, lam_init):
    q_blk = 3 * SB_WIDTH // LANES
    k_blk = q_blk + DA_HEADS
    v_blk = k_blk + DA_HEADS
    g_blk = SB_WIDTH // LANES
    vec = pl.BlockSpec((1, HEAD_DIM), lambda b, h: (0, 0))
    return pl.pallas_call(
        functools.partial(_da_kernel, lam_init=lam_init, seq=seq),
        grid=(batch // DA_BATCH, DA_HEADS),
        in_specs=[
            pl.BlockSpec(memory_space=pltpu.SMEM),
            pl.BlockSpec((DA_BATCH, seq, LANES), lambda b, h: (b, 0, q_blk + h)),
            pl.BlockSpec((DA_BATCH, seq, LANES), lambda b, h: (b, 0, k_blk + h)),
            pl.BlockSpec((DA_BATCH, seq, LANES), lambda b, h: (b, 0, v_blk + h)),
            pl.BlockSpec((DA_BATCH, seq, LANES), lambda b, h: (b, 0, g_blk + h)),
            vec, vec, vec, vec,
            pl.BlockSpec((1, LANES), lambda b, h: (0, 0)),
        ],
        out_specs=pl.BlockSpec((DA_BATCH, seq, LANES), lambda b, h: (b, 0, h)),
        out_shape=jax.ShapeDtypeStruct((batch, seq, DA_WIDTH), jnp.bfloat16),
        scratch_shapes=[
            pltpu.VMEM((DA_BATCH, seq, 2 * LANES), jnp.bfloat16),
            pltpu.VMEM((DA_BATCH, seq, 2 * LANES), jnp.bfloat16),
            pltpu.VMEM((2 * TQ, TK), jnp.float32),
            pltpu.VMEM((2, TQ, LANES), jnp.float32),
            pltpu.VMEM((2, TQ, 1), jnp.float32),
            pltpu.VMEM((2, TQ, 1), jnp.float32),
        ],
        compiler_params=pltpu.CompilerParams(
            dimension_semantics=("arbitrary", "arbitrary"), vmem_limit_bytes=VMEM_LIMIT),
        name="da_attention",
    )(params, qkv, qkv, qkv, gates, lq1, lk1, lq2, lk2, subln_g)


def kernel(x, norm_g, w_in, w_out, q_norm_g, k_norm_g, lambda_q1, lambda_k1, lambda_q2, lambda_k2, subln_g):
    batch, seq, d_model = x.shape
    depth = norm_g.shape[0]
    assert d_model == D_MODEL and seq % TQ == 0 and (batch * seq) % TM_PROJ == 0 and batch % DA_BATCH == 0
    slopes = jnp.asarray([2.0 ** (-8.0 * (h + 1) / DA_HEADS) for h in range(DA_HEADS)], jnp.float32)
    w_in_bf16 = w_in.astype(jnp.bfloat16)
    w_out_bf16 = w_out.astype(jnp.bfloat16)
    for l in range(depth):
        x2d = x.reshape(batch * seq, d_model)
        lam_init = 0.8 - 0.6 * math.exp(-0.3 * l)
        reps = SB_WIDTH // HEAD_DIM
        shift = HEAD_DIM * QK_SCALE * jnp.max(jnp.abs(q_norm_g[l])) * jnp.max(jnp.abs(k_norm_g[l]))
        params = jnp.concatenate([slopes, jnp.stack([shift, (shift < DA_SAFE_SHIFT).astype(jnp.float32)])])
        qkv, gates = _norm_proj(
            x2d, norm_g[l][None, :], w_in_bf16, l,
            jnp.tile(q_norm_g[l], reps)[None, :], jnp.tile(k_norm_g[l], reps)[None, :])
        qkv = qkv.reshape(batch, seq, -1)
        gates = gates.reshape(batch, seq, -1)
        da_mix = _da_attention(
            params, qkv, gates, lambda_q1[l][None, :], lambda_k1[l][None, :],
            lambda_q2[l][None, :], lambda_k2[l][None, :], subln_g[l][None, :], batch, seq, lam_init)
        x = _sb_attention_out(qkv, gates, da_mix, x, w_out_bf16, l, batch, seq)
    return x
```

```python
import functools
import math

import jax
import jax.numpy as jnp
from jax import lax
from jax.experimental import pallas as pl
from jax.experimental.pallas import tpu as pltpu

D_MODEL = 1024
CHUNK = 64
HEAD_DIM = 64
SB_WIDTH = 512
DA_HEADS = 4
DA_WIDTH = 512
PROJ_WIDTH = 4096
RMS_EPS = 1e-6
LANES = 128
NEG_BIG = -1e30
LOG2E = 1.4426950408889634
QK_SCALE = HEAD_DIM ** -0.5

TM_PROJ = 512
TQ = 256
TK = 256
VMEM_LIMIT = 56 * 1024 * 1024
DA_SAFE_SHIFT = 40.0
SB_DEAD_LOG2 = -151.0
DA_DEAD_BIAS = -110.0
DA_NEAR_BLOCKS = 2
DA_BATCH = 2

_NT = (((1,), (1,)), ((), ()))


def _dot(a, b):
    return jnp.dot(a, b, preferred_element_type=jnp.float32)


def _dot_nt(a, b):
    return lax.dot_general(a, b, _NT, preferred_element_type=jnp.float32)


def _silu(g):
    return g * jax.nn.sigmoid(g)


def _norm_proj_kernel(x_ref, g_ref, w_ref, qg_ref, kg_ref, qkv_ref, gate_ref, grp_ref):
    @pl.when(pl.program_id(0) == 0)
    def _():
        r = lax.broadcasted_iota(jnp.int32, (SB_WIDTH // 2, SB_WIDTH // 2), 0) // HEAD_DIM
        c = lax.broadcasted_iota(jnp.int32, (SB_WIDTH // 2, SB_WIDTH // 2), 1) // HEAD_DIM
        grp_ref[...] = jnp.where(r == c, 1.0, 0.0).astype(jnp.bfloat16)

    x = x_ref[...]
    ms = jnp.mean(x * x, axis=-1, keepdims=True)
    h = (x * lax.rsqrt(ms + RMS_EPS) * g_ref[...]).astype(jnp.bfloat16)

    def chunk(c):
        return _dot(h, w_ref[0, :, c * SB_WIDTH:(c + 1) * SB_WIDTH])

    def head_norm(y, gain):
        y2 = (y * y).astype(jnp.bfloat16)
        half = SB_WIDTH // 2
        ss = jnp.concatenate([_dot(y2[:, 0:half], grp_ref[...]), _dot(y2[:, half:], grp_ref[...])], axis=1)
        return y * lax.rsqrt(ss * (1.0 / HEAD_DIM) + RMS_EPS) * gain

    qkv_ref[:, 0 * SB_WIDTH:1 * SB_WIDTH] = (chunk(0) * (QK_SCALE * LOG2E)).astype(jnp.bfloat16)
    qkv_ref[:, 1 * SB_WIDTH:2 * SB_WIDTH] = chunk(1).astype(jnp.bfloat16)
    qkv_ref[:, 2 * SB_WIDTH:3 * SB_WIDTH] = chunk(2).astype(jnp.bfloat16)
    gate_ref[:, 0:SB_WIDTH] = chunk(3)
    qkv_ref[:, 3 * SB_WIDTH:4 * SB_WIDTH] = head_norm(chunk(4), qg_ref[...]).astype(jnp.bfloat16)
    qkv_ref[:, 4 * SB_WIDTH:5 * SB_WIDTH] = head_norm(chunk(5), kg_ref[...]).astype(jnp.bfloat16)
    qkv_ref[:, 5 * SB_WIDTH:6 * SB_WIDTH] = chunk(6).astype(jnp.bfloat16)
    gate_ref[:, SB_WIDTH:2 * SB_WIDTH] = chunk(7)


def _norm_proj(x2d, norm_g, w_bf16, layer, qg_t, kg_t):
    m = x2d.shape[0]
    return pl.pallas_call(
        _norm_proj_kernel,
        grid=(m // TM_PROJ,),
        in_specs=[
            pl.BlockSpec((TM_PROJ, D_MODEL), lambda i: (i, 0)),
            pl.BlockSpec((1, D_MODEL), lambda i: (0, 0)),
            pl.BlockSpec((1, D_MODEL, PROJ_WIDTH), lambda i: (layer, 0, 0)),
            pl.BlockSpec((1, SB_WIDTH), lambda i: (0, 0)),
            pl.BlockSpec((1, SB_WIDTH), lambda i: (0, 0)),
        ],
        out_specs=[
            pl.BlockSpec((TM_PROJ, 6 * SB_WIDTH), lambda i: (i, 0)),
            pl.BlockSpec((TM_PROJ, 2 * SB_WIDTH), lambda i: (i, 0)),
        ],
        out_shape=[
            jax.ShapeDtypeStruct((m, 6 * SB_WIDTH), jnp.bfloat16),
            jax.ShapeDtypeStruct((m, 2 * SB_WIDTH), jnp.float32),
        ],
        scratch_shapes=[pltpu.VMEM((SB_WIDTH // 2, SB_WIDTH // 2), jnp.bfloat16)],
        compiler_params=pltpu.CompilerParams(
            dimension_semantics=("arbitrary",), vmem_limit_bytes=VMEM_LIMIT),
        name="norm_proj",
    )(x2d, norm_g, w_bf16, qg_t, kg_t)


def _stack_halves(q, lane):
    zero = jnp.zeros_like(q)
    return jnp.concatenate([jnp.where(lane < HEAD_DIM, q, zero), jnp.where(lane >= HEAD_DIM, q, zero)], axis=0)


def _sb_kernel(q_ref, k_ref, v_ref, g_ref, da_ref, x_ref, wsb_ref, wda_ref, o_ref,
               ntri_ref, acc_ref, car_ref, *, seq):
    nq = seq // TQ

    @pl.when((pl.program_id(0) == 0) & (pl.program_id(1) == 0))
    def _():
        r = lax.broadcasted_iota(jnp.int32, (TK, TK), 0)
        c = lax.broadcasted_iota(jnp.int32, (TK, TK), 1)
        ntri_ref[...] = jnp.where(r > c, -1.0, 0.0).astype(jnp.bfloat16)

    lane = lax.broadcasted_iota(jnp.int32, (1, LANES), 1)
    row = lax.broadcasted_iota(jnp.int32, (2 * TQ, TK), 0) & (TQ - 1)
    col = lax.broadcasted_iota(jnp.int32, (2 * TQ, TK), 1)
    mask = col < row

    def stacked_q(qi):
        return _stack_halves(q_ref[0, qi * TQ:(qi + 1) * TQ, :], lane)

    def key_blocks(q2, first, last, acc, car, diagonal_first):
        for j in range(first, last - 1, -1):
            masked = diagonal_first and j == first
            keys = slice(j * TK, (j + 1) * TK)
            z = _dot_nt(q2, k_ref[0, keys, :])
            sp = jnp.maximum(z, 0.0) + jnp.log(1.0 + jnp.exp2(-jnp.abs(z))) * LOG2E
            if masked:
                sp = jnp.where(mask, sp, 0.0)
            tail = _dot(sp.astype(jnp.bfloat16), ntri_ref[...])
            arg = (z - sp) + tail
            if car is not None:
                arg = arg + car
            a = jnp.exp2(arg)
            if masked:
                a = jnp.where(mask, a, 0.0)
            d = _dot(a.astype(jnp.bfloat16), v_ref[0, keys, :])
            acc = d if acc is None else acc + d
            rs = jnp.sum(sp, axis=-1, keepdims=True)
            car = -rs if car is None else car - rs
        return acc, car

    alive = {}
    for qi in range(nq):
        acc, car = key_blocks(stacked_q(qi), qi, max(qi - 1, 0), None, None, True)
        if qi > 1:
            alive[qi] = jnp.max(car) >= SB_DEAD_LOG2
            car_ref[qi] = car
        acc_ref[qi] = acc

    for qi in range(2, nq):
        @pl.when(alive[qi])
        def _(qi=qi):
            acc, _ = key_blocks(stacked_q(qi), qi - 2, 0, acc_ref[qi], car_ref[qi], False)
            acc_ref[qi] = acc

    def project(first):
        w_pair = jnp.concatenate([wsb_ref[0], wda_ref[0]], axis=0)
        for qi in range(nq):
            rows = slice(qi * TQ, (qi + 1) * TQ)
            out = jnp.where(lane < HEAD_DIM, acc_ref[qi, 0:TQ, :], acc_ref[qi, TQ:2 * TQ, :])
            sb = (out * _silu(g_ref[0, rows, :])).astype(jnp.bfloat16)
            y = _dot(jnp.concatenate([sb, da_ref[0, rows, :]], axis=1), w_pair)
            base = x_ref[0, rows, :] if first else o_ref[0, rows, :]
            o_ref[0, rows, :] = base + y

    @pl.when(pl.program_id(1) == 0)
    def _():
        project(True)

    @pl.when(pl.program_id(1) != 0)
    def _():
        project(False)


def _sb_attention_out(qkv, gates, da_mix, x, w_out_bf16, layer, batch, seq):
    hp = SB_WIDTH // LANES
    assert DA_HEADS == hp
    return pl.pallas_call(
        functools.partial(_sb_kernel, seq=seq),
        grid=(batch, hp),
        in_specs=[
            pl.BlockSpec((1, seq, LANES), lambda b, p: (b, 0, p)),
            pl.BlockSpec((1, seq, LANES), lambda b, p: (b, 0, hp + p)),
            pl.BlockSpec((1, seq, LANES), lambda b, p: (b, 0, 2 * hp + p)),
            pl.BlockSpec((1, seq, LANES), lambda b, p: (b, 0, p)),
            pl.BlockSpec((1, seq, LANES), lambda b, p: (b, 0, p)),
            pl.BlockSpec((1, seq, D_MODEL), lambda b, p: (b, 0, 0)),
            pl.BlockSpec((1, LANES, D_MODEL), lambda b, p: (layer, p, 0)),
            pl.BlockSpec((1, LANES, D_MODEL), lambda b, p: (layer, hp + p, 0)),
        ],
        out_specs=pl.BlockSpec((1, seq, D_MODEL), lambda b, p: (b, 0, 0)),
        out_shape=jax.ShapeDtypeStruct((batch, seq, D_MODEL), jnp.float32),
        scratch_shapes=[
            pltpu.VMEM((TK, TK), jnp.bfloat16),
            pltpu.VMEM((seq // TQ, 2 * TQ, LANES), jnp.float32),
            pltpu.VMEM((seq // TQ, 2 * TQ, 1), jnp.float32),
        ],
        compiler_params=pltpu.CompilerParams(
            dimension_semantics=("arbitrary", "arbitrary"), vmem_limit_bytes=VMEM_LIMIT),
        name="sb_attention_out",
    )(qkv, qkv, qkv, gates, da_mix, x, w_out_bf16, w_out_bf16)


_X_ROW, _X_Q0, _X_SHIFT, _X_COL, _X_K0 = 0, 1, 2, 3, 4


def _da_kernel(par_ref, q_ref, k_ref, v_ref, g_ref, lq1_ref, lk1_ref, lq2_ref, lk2_ref, sg_ref, o_ref,
               kx_ref, vx_ref, dc_ref, sacc_ref, m_ref, l_ref, *, lam_init, seq):
    nq = seq // TQ
    slope = par_ref[pl.program_id(1)]
    shift = par_ref[DA_HEADS]
    safe = par_ref[DA_HEADS + 1]
    lane = lax.broadcasted_iota(jnp.int32, (1, LANES), 1)
    scale = jnp.asarray(QK_SCALE, jnp.bfloat16)

    def finish(bb, rows, o):
        lam = (jnp.exp(jnp.sum(lq1_ref[...] * lk1_ref[...], axis=-1, keepdims=True))
               - jnp.exp(jnp.sum(lq2_ref[...] * lk2_ref[...], axis=-1, keepdims=True)) + lam_init)
        d = o[0] - lam * o[1]
        d = d * lax.rsqrt(jnp.mean(d * d, axis=-1, keepdims=True) + RMS_EPS) * sg_ref[...]
        d = d * (1.0 - lam_init)
        o_ref[bb, rows, :] = (d * _silu(g_ref[bb, rows, :])).astype(o_ref.dtype)

    def fixed_shift(skip_far):
        pos = lax.broadcasted_iota(jnp.int32, (seq, 1), 0)
        colf = (pos & (TK - 1)).astype(jnp.float32)
        blkf = (pos - (pos & (TK - 1))).astype(jnp.float32)
        kx = jnp.where(lane <= _X_SHIFT, 1.0,
                       jnp.where(lane == _X_COL, slope * colf, jnp.where(lane == _X_K0, slope * blkf, 0.0)))
        for bb in range(DA_BATCH):
            kx_ref[bb, :, 0:LANES] = k_ref[bb]
            kx_ref[bb, :, LANES:2 * LANES] = kx.astype(jnp.bfloat16)
            vx_ref[bb, :, 0:LANES] = v_ref[bb]
            vx_ref[bb, :, LANES:2 * LANES] = jnp.ones((seq, LANES), jnp.bfloat16)
        r = lax.broadcasted_iota(jnp.int32, (2 * TQ, TK), 0) & (TQ - 1)
        c = lax.broadcasted_iota(jnp.int32, (2 * TQ, TK), 1)
        dc_ref[...] = jnp.where(c <= r, 0.0, jnp.where((c // CHUNK) == (r // CHUNK),
                                                       (-2.0 * slope) * (c - r).astype(jnp.float32), NEG_BIG))
        rowf = lax.broadcasted_iota(jnp.int32, (TQ, 1), 0).astype(jnp.float32)

        def stacked_q(bb, qi):
            qx = jnp.where(lane == _X_ROW, -slope * rowf,
                           jnp.where(lane == _X_Q0, -slope * float(qi * TQ),
                                     jnp.where(lane == _X_SHIFT, -shift,
                                               jnp.where((lane == _X_COL) | (lane == _X_K0), 1.0, 0.0))))
            qx = qx.astype(jnp.bfloat16)
            qs = _stack_halves(q_ref[bb, qi * TQ:(qi + 1) * TQ, :] * scale, lane)
            return jnp.concatenate([qs, jnp.concatenate([qx, qx], axis=0)], axis=1)

        def key_blocks(bb, q2, first, last, acc, diagonal_first):
            for j in range(first, last - 1, -1):
                keys = slice(j * TK, (j + 1) * TK)
                s = _dot_nt(q2, kx_ref[bb, keys, :])
                if diagonal_first and j == first:
                    s = s + dc_ref[...]
                p = jnp.exp(s).astype(jnp.bfloat16)
                d = _dot(p, vx_ref[bb, keys, :])
                acc = d if acc is None else acc + d
            return acc

        def normalised(acc):
            return (acc[0:TQ, 0:LANES] / acc[0:TQ, LANES:2 * LANES],
                    acc[TQ:2 * TQ, 0:LANES] / acc[TQ:2 * TQ, LANES:2 * LANES])

        for bb in range(DA_BATCH):
            for qi in range(nq):
                last = max(qi - DA_NEAR_BLOCKS, 0) if skip_far else 0
                acc = key_blocks(bb, stacked_q(bb, qi), qi, last, None, True)
                finish(bb, slice(qi * TQ, (qi + 1) * TQ), normalised(acc))

    far_dead = slope * float(DA_NEAR_BLOCKS * TK + 1) > -DA_DEAD_BIAS

    @pl.when((safe > 0.5) & far_dead)
    def _():
        fixed_shift(True)

    @pl.when((safe > 0.5) & jnp.logical_not(far_dead))
    def _():
        fixed_shift(False)

    @pl.when(safe <= 0.5)
    def _():
        row = lax.broadcasted_iota(jnp.int32, (TQ, TK), 0)
        col = lax.broadcasted_iota(jnp.int32, (TQ, TK), 1)

        def q_block(idx, carry):
            bb = idx // nq
            qi = idx - bb * nq
            q0 = pl.multiple_of(qi * TQ, TQ)
            q = q_ref[bb, pl.ds(q0, TQ), :] * scale
            zero = jnp.zeros_like(q)
            q_halves = (jnp.where(lane < HEAD_DIM, q, zero), jnp.where(lane >= HEAD_DIM, q, zero))
            sacc_ref[...] = jnp.zeros_like(sacc_ref)
            l_ref[...] = jnp.zeros_like(l_ref)
            m_ref[...] = jnp.full_like(m_ref, NEG_BIG)

            def block(j, diagonal):
                start = pl.multiple_of(j * TK, TK)
                k = k_ref[bb, pl.ds(start, TK), :]
                v = v_ref[bb, pl.ds(start, TK), :]
                dist = (row - col + (qi - j) * TK).astype(jnp.float32)
                bias = -slope * jnp.abs(dist)
                if diagonal:
                    bias = jnp.where((col // CHUNK) <= (row // CHUNK), bias, NEG_BIG)
                for c in range(2):
                    s = _dot_nt(q_halves[c], k) + bias
                    m_old = m_ref[c]
                    m_new = jnp.maximum(m_old, jnp.max(s, axis=-1, keepdims=True))
                    p = jnp.exp(s - m_new)
                    alpha = jnp.exp(m_old - m_new)
                    l_ref[c] = alpha * l_ref[c] + jnp.sum(p, axis=-1, keepdims=True)
                    sacc_ref[c] = alpha * sacc_ref[c] + _dot(p.astype(jnp.bfloat16), v)
                    m_ref[c] = m_new

            block(qi, True)

            def body(it, c):
                block(qi - 1 - it, False)
                return c

            lax.fori_loop(0, qi, body, 0)
            finish(bb, pl.ds(q0, TQ), (sacc_ref[0] / l_ref[0], sacc_ref[1] / l_ref[1]))
            return carry

        lax.fori_loop(0, DA_BATCH * nq, q_block, 0)


def _da_attention(params, qkv, gates, lq1, lk1, lq2, lk2, subln_g, batch, seq, lam_init):
    q_blk = 3 * SB_WIDTH // LANES
    k_blk = q_blk + DA_HEADS
    v_blk = k_blk + DA_HEADS
    g_blk = SB_WIDTH // LANES
    vec = pl.BlockSpec((1, HEAD_DIM), lambda b, h: (0, 0))
    return pl.pallas_call(
        functools.partial(_da_kernel, lam_init=lam_init, seq=seq),
        grid=(batch // DA_BATCH, DA_HEADS),
        in_specs=[
            pl.BlockSpec(memory_space=pltpu.SMEM),
            pl.BlockSpec((DA_BATCH, seq, LANES), lambda b, h: (b, 0, q_blk + h)),
            pl.BlockSpec((DA_BATCH, seq, LANES), lambda b, h: (b, 0, k_blk + h)),
            pl.BlockSpec((DA_BATCH, seq, LANES), lambda b, h: (b, 0, v_blk + h)),
            pl.BlockSpec((DA_BATCH, seq, LANES), lambda b, h: (b, 0, g_blk + h)),
            vec, vec, vec, vec,
            pl.BlockSpec((1, LANES), lambda b, h: (0, 0)),
        ],
        out_specs=pl.BlockSpec((DA_BATCH, seq, LANES), lambda b, h: (b, 0, h)),
        out_shape=jax.ShapeDtypeStruct((batch, seq, DA_WIDTH), jnp.bfloat16),
        scratch_shapes=[
            pltpu.VMEM((DA_BATCH, seq, 2 * LANES), jnp.bfloat16),
            pltpu.VMEM((DA_BATCH, seq, 2 * LANES), jnp.bfloat16),
            pltpu.VMEM((2 * TQ, TK), jnp.float32),
            pltpu.VMEM((2, TQ, LANES), jnp.float32),
            pltpu.VMEM((2, TQ, 1), jnp.float32),
            pltpu.VMEM((2, TQ, 1), jnp.float32),
        ],
        compiler_params=pltpu.CompilerParams(
            dimension_semantics=("arbitrary", "arbitrary"), vmem_limit_bytes=VMEM_LIMIT),
        name="da_attention",
    )(params, qkv, qkv, qkv, gates, lq1, lk1, lq2, lk2, subln_g)


def kernel(x, norm_g, w_in, w_out, q_norm_g, k_norm_g, lambda_q1, lambda_k1, lambda_q2, lambda_k2, subln_g):
    batch, seq, d_model = x.shape
    depth = norm_g.shape[0]
    assert d_model == D_MODEL and seq % TQ == 0 and (batch * seq) % TM_PROJ == 0 and batch % DA_BATCH == 0
    slopes = jnp.asarray([2.0 ** (-8.0 * (h + 1) / DA_HEADS) for h in range(DA_HEADS)], jnp.float32)
    w_in_bf16 = w_in.astype(jnp.bfloat16)
    w_out_bf16 = w_out.astype(jnp.bfloat16)
    for l in range(depth):
        x2d = x.reshape(batch * seq, d_model)
        lam_init = 0.8 - 0.6 * math.exp(-0.3 * l)
        reps = SB_WIDTH // HEAD_DIM
        shift = HEAD_DIM * QK_SCALE * jnp.max(jnp.abs(q_norm_g[l])) * jnp.max(jnp.abs(k_norm_g[l]))
        params = jnp.concatenate([slopes, jnp.stack([shift, (shift < DA_SAFE_SHIFT).astype(jnp.float32)])])
        qkv, gates = _norm_proj(
            x2d, norm_g[l][None, :], w_in_bf16, l,
            jnp.tile(q_norm_g[l], reps)[None, :], jnp.tile(k_norm_g[l], reps)[None, :])
        qkv = qkv.reshape(batch, seq, -1)
        gates = gates.reshape(batch, seq, -1)
        da_mix = _da_attention(
            params, qkv, gates, lambda_q1[l][None, :], lambda_k1[l][None, :],
            lambda_q2[l][None, :], lambda_k2[l][None, :], subln_g[l][None, :], batch, seq, lam_init)
        x = _sb_attention_out(qkv, gates, da_mix, x, w_out_bf16, l, batch, seq)
    return x
```

```python
import functools
import math

import jax
import jax.numpy as jnp
from jax import lax
from jax.experimental import pallas as pl
from jax.experimental.pallas import tpu as pltpu

D_MODEL = 1024
CHUNK = 64
HEAD_DIM = 64
SB_WIDTH = 512
DA_HEADS = 4
DA_WIDTH = 512
PROJ_WIDTH = 4096
RMS_EPS = 1e-6
LANES = 128
NEG_BIG = -1e30
LOG2E = 1.4426950408889634
QK_SCALE = HEAD_DIM ** -0.5

TM_PROJ = 512
TQ = 256
TK = 256
VMEM_LIMIT = 56 * 1024 * 1024
DA_SAFE_SHIFT = 40.0
SB_DEAD_LOG2 = -151.0
DA_DEAD_BIAS = -110.0
DA_NEAR_BLOCKS = 2
DA_BATCH = 4

_NT = (((1,), (1,)), ((), ()))


def _dot(a, b):
    return jnp.dot(a, b, preferred_element_type=jnp.float32)


def _dot_nt(a, b):
    return lax.dot_general(a, b, _NT, preferred_element_type=jnp.float32)


def _silu(g):
    return g * jax.nn.sigmoid(g)


def _norm_proj_kernel(x_ref, g_ref, w_ref, qg_ref, kg_ref, qkv_ref, gate_ref, grp_ref):
    @pl.when(pl.program_id(0) == 0)
    def _():
        r = lax.broadcasted_iota(jnp.int32, (SB_WIDTH // 2, SB_WIDTH // 2), 0) // HEAD_DIM
        c = lax.broadcasted_iota(jnp.int32, (SB_WIDTH // 2, SB_WIDTH // 2), 1) // HEAD_DIM
        grp_ref[...] = jnp.where(r == c, 1.0, 0.0).astype(jnp.bfloat16)

    x = x_ref[...]
    ms = jnp.mean(x * x, axis=-1, keepdims=True)
    h = (x * lax.rsqrt(ms + RMS_EPS) * g_ref[...]).astype(jnp.bfloat16)

    def chunk(c):
        return _dot(h, w_ref[0, :, c * SB_WIDTH:(c + 1) * SB_WIDTH])

    def head_norm(y, gain):
        y2 = (y * y).astype(jnp.bfloat16)
        half = SB_WIDTH // 2
        ss = jnp.concatenate([_dot(y2[:, 0:half], grp_ref[...]), _dot(y2[:, half:], grp_ref[...])], axis=1)
        return y * lax.rsqrt(ss * (1.0 / HEAD_DIM) + RMS_EPS) * gain

    qkv_ref[:, 0 * SB_WIDTH:1 * SB_WIDTH] = (chunk(0) * (QK_SCALE * LOG2E)).astype(jnp.bfloat16)
    qkv_ref[:, 1 * SB_WIDTH:2 * SB_WIDTH] = chunk(1).astype(jnp.bfloat16)
    qkv_ref[:, 2 * SB_WIDTH:3 * SB_WIDTH] = chunk(2).astype(jnp.bfloat16)
    gate_ref[:, 0:SB_WIDTH] = chunk(3)
    qkv_ref[:, 3 * SB_WIDTH:4 * SB_WIDTH] = head_norm(chunk(4), qg_ref[...]).astype(jnp.bfloat16)
    qkv_ref[:, 4 * SB_WIDTH:5 * SB_WIDTH] = head_norm(chunk(5), kg_ref[...]).astype(jnp.bfloat16)
    qkv_ref[:, 5 * SB_WIDTH:6 * SB_WIDTH] = chunk(6).astype(jnp.bfloat16)
    gate_ref[:, SB_WIDTH:2 * SB_WIDTH] = chunk(7)


def _norm_proj(x2d, norm_g, w_bf16, layer, qg_t, kg_t):
    m = x2d.shape[0]
    return pl.pallas_call(
        _norm_proj_kernel,
        grid=(m // TM_PROJ,),
        in_specs=[
            pl.BlockSpec((TM_PROJ, D_MODEL), lambda i: (i, 0)),
            pl.BlockSpec((1, D_MODEL), lambda i: (0, 0)),
            pl.BlockSpec((1, D_MODEL, PROJ_WIDTH), lambda i: (layer, 0, 0)),
            pl.BlockSpec((1, SB_WIDTH), lambda i: (0, 0)),
            pl.BlockSpec((1, SB_WIDTH), lambda i: (0, 0)),
        ],
        out_specs=[
            pl.BlockSpec((TM_PROJ, 6 * SB_WIDTH), lambda i: (i, 0)),
            pl.BlockSpec((TM_PROJ, 2 * SB_WIDTH), lambda i: (i, 0)),
        ],
        out_shape=[
            jax.ShapeDtypeStruct((m, 6 * SB_WIDTH), jnp.bfloat16),
            jax.ShapeDtypeStruct((m, 2 * SB_WIDTH), jnp.float32),
        ],
        scratch_shapes=[pltpu.VMEM((SB_WIDTH // 2, SB_WIDTH // 2), jnp.bfloat16)],
        compiler_params=pltpu.CompilerParams(
            dimension_semantics=("arbitrary",), vmem_limit_bytes=VMEM_LIMIT),
        name="norm_proj",
    )(x2d, norm_g, w_bf16, qg_t, kg_t)


def _stack_halves(q, lane):
    zero = jnp.zeros_like(q)
    return jnp.concatenate([jnp.where(lane < HEAD_DIM, q, zero), jnp.where(lane >= HEAD_DIM, q, zero)], axis=0)


def _sb_kernel(q_ref, k_ref, v_ref, g_ref, da_ref, x_ref, wsb_ref, wda_ref, o_ref,
               ntri_ref, acc_ref, car_ref, *, seq):
    nq = seq // TQ

    @pl.when((pl.program_id(0) == 0) & (pl.program_id(1) == 0))
    def _():
        r = lax.broadcasted_iota(jnp.int32, (TK, TK), 0)
        c = lax.broadcasted_iota(jnp.int32, (TK, TK), 1)
        ntri_ref[...] = jnp.where(r > c, -1.0, 0.0).astype(jnp.bfloat16)

    lane = lax.broadcasted_iota(jnp.int32, (1, LANES), 1)
    row = lax.broadcasted_iota(jnp.int32, (2 * TQ, TK), 0) & (TQ - 1)
    col = lax.broadcasted_iota(jnp.int32, (2 * TQ, TK), 1)
    mask = col < row

    def stacked_q(qi):
        return _stack_halves(q_ref[0, qi * TQ:(qi + 1) * TQ, :], lane)

    def key_blocks(q2, first, last, acc, car, diagonal_first):
        for j in range(first, last - 1, -1):
            masked = diagonal_first and j == first
            keys = slice(j * TK, (j + 1) * TK)
            z = _dot_nt(q2, k_ref[0, keys, :])
            sp = jnp.maximum(z, 0.0) + jnp.log(1.0 + jnp.exp2(-jnp.abs(z))) * LOG2E
            if masked:
                sp = jnp.where(mask, sp, 0.0)
            tail = _dot(sp.astype(jnp.bfloat16), ntri_ref[...])
            arg = (z - sp) + tail
            if car is not None:
                arg = arg + car
            a = jnp.exp2(arg)
            if masked:
                a = jnp.where(mask, a, 0.0)
            d = _dot(a.astype(jnp.bfloat16), v_ref[0, keys, :])
            acc = d if acc is None else acc + d
            rs = jnp.sum(sp, axis=-1, keepdims=True)
            car = -rs if car is None else car - rs
        return acc, car

    alive = {}
    for qi in range(nq):
        acc, car = key_blocks(stacked_q(qi), qi, max(qi - 1, 0), None, None, True)
        if qi > 1:
            alive[qi] = jnp.max(car) >= SB_DEAD_LOG2
            car_ref[qi] = car
        acc_ref[qi] = acc

    for qi in range(2, nq):
        @pl.when(alive[qi])
        def _(qi=qi):
            acc, _ = key_blocks(stacked_q(qi), qi - 2, 0, acc_ref[qi], car_ref[qi], False)
            acc_ref[qi] = acc

    def project(first):
        w_pair = jnp.concatenate([wsb_ref[0], wda_ref[0]], axis=0)
        for qi in range(nq):
            rows = slice(qi * TQ, (qi + 1) * TQ)
            out = jnp.where(lane < HEAD_DIM, acc_ref[qi, 0:TQ, :], acc_ref[qi, TQ:2 * TQ, :])
            sb = (out * _silu(g_ref[0, rows, :])).astype(jnp.bfloat16)
            y = _dot(jnp.concatenate([sb, da_ref[0, rows, :]], axis=1), w_pair)
            base = x_ref[0, rows, :] if first else o_ref[0, rows, :]
            o_ref[0, rows, :] = base + y

    @pl.when(pl.program_id(1) == 0)
    def _():
        project(True)

    @pl.when(pl.program_id(1) != 0)
    def _():
        project(False)


def _sb_attention_out(qkv, gates, da_mix, x, w_out_bf16, layer, batch, seq):
    hp = SB_WIDTH // LANES
    assert DA_HEADS == hp
    return pl.pallas_call(
        functools.partial(_sb_kernel, seq=seq),
        grid=(batch, hp),
        in_specs=[
            pl.BlockSpec((1, seq, LANES), lambda b, p: (b, 0, p)),
            pl.BlockSpec((1, seq, LANES), lambda b, p: (b, 0, hp + p)),
            pl.BlockSpec((1, seq, LANES), lambda b, p: (b, 0, 2 * hp + p)),
            pl.BlockSpec((1, seq, LANES), lambda b, p: (b, 0, p)),
            pl.BlockSpec((1, seq, LANES), lambda b, p: (b, 0, p)),
            pl.BlockSpec((1, seq, D_MODEL), lambda b, p: (b, 0, 0)),
            pl.BlockSpec((1, LANES, D_MODEL), lambda b, p: (layer, p, 0)),
            pl.BlockSpec((1, LANES, D_MODEL), lambda b, p: (layer, hp + p, 0)),
        ],
        out_specs=pl.BlockSpec((1, seq, D_MODEL), lambda b, p: (b, 0, 0)),
        out_shape=jax.ShapeDtypeStruct((batch, seq, D_MODEL), jnp.float32),
        scratch_shapes=[
            pltpu.VMEM((TK, TK), jnp.bfloat16),
            pltpu.VMEM((seq // TQ, 2 * TQ, LANES), jnp.float32),
            pltpu.VMEM((seq // TQ, 2 * TQ, 1), jnp.float32),
        ],
        compiler_params=pltpu.CompilerParams(
            dimension_semantics=("arbitrary", "arbitrary"), vmem_limit_bytes=VMEM_LIMIT),
        name="sb_attention_out",
    )(qkv, qkv, qkv, gates, da_mix, x, w_out_bf16, w_out_bf16)


_X_ROW, _X_Q0, _X_SHIFT, _X_COL, _X_K0 = 0, 1, 2, 3, 4


def _da_kernel(par_ref, q_ref, k_ref, v_ref, g_ref, lq1_ref, lk1_ref, lq2_ref, lk2_ref, sg_ref, o_ref,
               kx_ref, vx_ref, dc_ref, sacc_ref, m_ref, l_ref, *, lam_init, seq):
    nq = seq // TQ
    slope = par_ref[pl.program_id(1)]
    shift = par_ref[DA_HEADS]
    safe = par_ref[DA_HEADS + 1]
    lane = lax.broadcasted_iota(jnp.int32, (1, LANES), 1)
    scale = jnp.asarray(QK_SCALE, jnp.bfloat16)

    def finish(bb, rows, o):
        lam = (jnp.exp(jnp.sum(lq1_ref[...] * lk1_ref[...], axis=-1, keepdims=True))
               - jnp.exp(jnp.sum(lq2_ref[...] * lk2_ref[...], axis=-1, keepdims=True)) + lam_init)
        d = o[0] - lam * o[1]
        d = d * lax.rsqrt(jnp.mean(d * d, axis=-1, keepdims=True) + RMS_EPS) * sg_ref[...]
        d = d * (1.0 - lam_init)
        o_ref[bb, rows, :] = (d * _silu(g_ref[bb, rows, :])).astype(o_ref.dtype)

    def fixed_shift(skip_far):
        pos = lax.broadcasted_iota(jnp.int32, (seq, 1), 0)
        colf = (pos & (TK - 1)).astype(jnp.float32)
        blkf = (pos - (pos & (TK - 1))).astype(jnp.float32)
        kx = jnp.where(lane <= _X_SHIFT, 1.0,
                       jnp.where(lane == _X_COL, slope * colf, jnp.where(lane == _X_K0, slope * blkf, 0.0)))
        for bb in range(DA_BATCH):
            kx_ref[bb, :, 0:LANES] = k_ref[bb]
            kx_ref[bb, :, LANES:2 * LANES] = kx.astype(jnp.bfloat16)
            vx_ref[bb, :, 0:LANES] = v_ref[bb]
            vx_ref[bb, :, LANES:2 * LANES] = jnp.ones((seq, LANES), jnp.bfloat16)
        r = lax.broadcasted_iota(jnp.int32, (2 * TQ, TK), 0) & (TQ - 1)
        c = lax.broadcasted_iota(jnp.int32, (2 * TQ, TK), 1)
        dc_ref[...] = jnp.where(c <= r, 0.0, jnp.where((c // CHUNK) == (r // CHUNK),
                                                       (-2.0 * slope) * (c - r).astype(jnp.float32), NEG_BIG))
        rowf = lax.broadcasted_iota(jnp.int32, (TQ, 1), 0).astype(jnp.float32)

        def stacked_q(bb, qi):
            qx = jnp.where(lane == _X_ROW, -slope * rowf,
                           jnp.where(lane == _X_Q0, -slope * float(qi * TQ),
                                     jnp.where(lane == _X_SHIFT, -shift,
                                               jnp.where((lane == _X_COL) | (lane == _X_K0), 1.0, 0.0))))
            qx = qx.astype(jnp.bfloat16)
            qs = _stack_halves(q_ref[bb, qi * TQ:(qi + 1) * TQ, :] * scale, lane)
            return jnp.concatenate([qs, jnp.concatenate([qx, qx], axis=0)], axis=1)

        def key_blocks(bb, q2, first, last, acc, diagonal_first):
            for j in range(first, last - 1, -1):
                keys = slice(j * TK, (j + 1) * TK)
                s = _dot_nt(q2, kx_ref[bb, keys, :])
                if diagonal_first and j == first:
                    s = s + dc_ref[...]
                p = jnp.exp(s).astype(jnp.bfloat16)
                d = _dot(p, vx_ref[bb, keys, :])
                acc = d if acc is None else acc + d
            return acc

        def normalised(acc):
            return (acc[0:TQ, 0:LANES] / acc[0:TQ, LANES:2 * LANES],
                    acc[TQ:2 * TQ, 0:LANES] / acc[TQ:2 * TQ, LANES:2 * LANES])

        for bb in range(DA_BATCH):
            for qi in range(nq):
                last = max(qi - DA_NEAR_BLOCKS, 0) if skip_far else 0
                acc = key_blocks(bb, stacked_q(bb, qi), qi, last, None, True)
                finish(bb, slice(qi * TQ, (qi + 1) * TQ), normalised(acc))

    far_dead = slope * float(DA_NEAR_BLOCKS * TK + 1) > -DA_DEAD_BIAS

    @pl.when((safe > 0.5) & far_dead)
    def _():
        fixed_shift(True)

    @pl.when((safe > 0.5) & jnp.logical_not(far_dead))
    def _():
        fixed_shift(False)

    @pl.when(safe <= 0.5)
    def _():
        row = lax.broadcasted_iota(jnp.int32, (TQ, TK), 0)
        col = lax.broadcasted_iota(jnp.int32, (TQ, TK), 1)

        def q_block(idx, carry):
            bb = idx // nq
            qi = idx - bb * nq
            q0 = pl.multiple_of(qi * TQ, TQ)
            q = q_ref[bb, pl.ds(q0, TQ), :] * scale
            zero = jnp.zeros_like(q)
            q_halves = (jnp.where(lane < HEAD_DIM, q, zero), jnp.where(lane >= HEAD_DIM, q, zero))
            sacc_ref[...] = jnp.zeros_like(sacc_ref)
            l_ref[...] = jnp.zeros_like(l_ref)
            m_ref[...] = jnp.full_like(m_ref, NEG_BIG)

            def block(j, diagonal):
                start = pl.multiple_of(j * TK, TK)
                k = k_ref[bb, pl.ds(start, TK), :]
                v = v_ref[bb, pl.ds(start, TK), :]
                dist = (row - col + (qi - j) * TK).astype(jnp.float32)
                bias = -slope * jnp.abs(dist)
                if diagonal:
                    bias = jnp.where((col // CHUNK) <= (row // CHUNK), bias, NEG_BIG)
                for c in range(2):
                    s = _dot_nt(q_halves[c], k) + bias
                    m_old = m_ref[c]
                    m_new = jnp.maximum(m_old, jnp.max(s, axis=-1, keepdims=True))
                    p = jnp.exp(s - m_new)
                    alpha = jnp.exp(m_old - m_new)
                    l_ref[c] = alpha * l_ref[c] + jnp.sum(p, axis=-1, keepdims=True)
                    sacc_ref[c] = alpha * sacc_ref[c] + _dot(p.astype(jnp.bfloat16), v)
                    m_ref[c] = m_new

            block(qi, True)

            def body(it, c):
                block(qi - 1 - it, False)
                return c

            lax.fori_loop(0, qi, body, 0)
            finish(bb, pl.ds(q0, TQ), (sacc_ref[0] / l_ref[0], sacc_ref[1] / l_ref[1]))
            return carry

        lax.fori_loop(0, DA_BATCH * nq, q_block, 0)


def _da_attention(params, qkv, gates, lq1, lk1, lq2, lk2, subln_g, batch, seq, lam_init):
    q_blk = 3 * SB_WIDTH // LANES
    k_blk = q_blk + DA_HEADS
    v_blk = k_blk + DA_HEADS
    g_blk = SB_WIDTH // LANES
    vec = pl.BlockSpec((1, HEAD_DIM), lambda b, h: (0, 0))
    return pl.pallas_call(
        functools.partial(_da_kernel, lam_init=lam_init, seq=seq),
        grid=(batch // DA_BATCH, DA_HEADS),
        in_specs=[
            pl.BlockSpec(memory_space=pltpu.SMEM),
            pl.BlockSpec((DA_BATCH, seq, LANES), lambda b, h: (b, 0, q_blk + h)),
            pl.BlockSpec((DA_BATCH, seq, LANES), lambda b, h: (b, 0, k_blk + h)),
            pl.BlockSpec((DA_BATCH, seq, LANES), lambda b, h: (b, 0, v_blk + h)),
            pl.BlockSpec((DA_BATCH, seq, LANES), lambda b, h: (b, 0, g_blk + h)),
            vec, vec, vec, vec,
            pl.BlockSpec((1, LANES), lambda b, h: (0, 0)),
        ],
        out_specs=pl.BlockSpec((DA_BATCH, seq, LANES), lambda b, h: (b, 0, h)),
        out_shape=jax.ShapeDtypeStruct((batch, seq, DA_WIDTH), jnp.bfloat16),
        scratch_shapes=[
            pltpu.VMEM((DA_BATCH, seq, 2 * LANES), jnp.bfloat16),
            pltpu.VMEM((DA_BATCH, seq, 2 * LANES), jnp.bfloat16),
            pltpu.VMEM((2 * TQ, TK), jnp.float32),
            pltpu.VMEM((2, TQ, LANES), jnp.float32),
            pltpu.VMEM((2, TQ, 1), jnp.float32),
            pltpu.VMEM((2, TQ, 1), jnp.float32),
        ],
        compiler_params=pltpu.CompilerParams(
            dimension_semantics=("arbitrary", "arbitrary"), vmem_limit_bytes=VMEM_LIMIT),
        name="da_attention",
    )(params, qkv, qkv, qkv, gates, lq1, lk1, lq2, lk2, subln_g)


def kernel(x, norm_g, w_in, w_out, q_norm_g, k_norm_g, lambda_q1, lambda_k1, lambda_q2, lambda_k2, subln_g):
    batch, seq, d_model = x.shape
    depth = norm_g.shape[0]
    assert d_model == D_MODEL and seq % TQ == 0 and (batch * seq) % TM_PROJ == 0 and batch % DA_BATCH == 0
    slopes = jnp.asarray([2.0 ** (-8.0 * (h + 1) / DA_HEADS) for h in range(DA_HEADS)], jnp.float32)
    w_in_bf16 = w_in.astype(jnp.bfloat16)
    w_out_bf16 = w_out.astype(jnp.bfloat16)
    for l in range(depth):
        x2d = x.reshape(batch * seq, d_model)
        lam_init = 0.8 - 0.6 * math.exp(-0.3 * l)
        reps = SB_WIDTH // HEAD_DIM
        shift = HEAD_DIM * QK_SCALE * jnp.max(jnp.abs(q_norm_g[l])) * jnp.max(jnp.abs(k_norm_g[l]))
        params = jnp.concatenate([slopes, jnp.stack([shift, (shift < DA_SAFE_SHIFT).astype(jnp.float32)])])
        qkv, gates = _norm_proj(
            x2d, norm_g[l][None, :], w_in_bf16, l,
            jnp.tile(q_norm_g[l], reps)[None, :], jnp.tile(k_norm_g[l], reps)[None, :])
        qkv = qkv.reshape(batch, seq, -1)
        gates = gates.reshape(batch, seq, -1)
        da_mix = _da_attention(
            params, qkv, gates, lambda_q1[l][None, :], lambda_k1[l][None, :],
            lambda_q2[l][None, :], lambda_k2[l][None, :], subln_g[l][None, :], batch, seq, lam_init)
        x = _sb_attention_out(qkv, gates, da_mix, x, w_out_bf16, l, batch, seq)
    return x
```

```python
import functools
import math

import jax
import jax.numpy as jnp
from jax import lax
from jax.experimental import pallas as pl
from jax.experimental.pallas import tpu as pltpu

D_MODEL = 1024
CHUNK = 64
HEAD_DIM = 64
SB_WIDTH = 512
DA_HEADS = 4
DA_WIDTH = 512
PROJ_WIDTH = 4096
RMS_EPS = 1e-6
LANES = 128
NEG_BIG = -1e30
LOG2E = 1.4426950408889634
QK_SCALE = HEAD_DIM ** -0.5

TM_PROJ = 512
TQ = 256
TK = 256
VMEM_LIMIT = 56 * 1024 * 1024
DA_SAFE_SHIFT = 40.0
SB_DEAD_LOG2 = -151.0
DA_DEAD_BIAS = -110.0
DA_NEAR_BLOCKS = 2
DA_BATCH = 2

_NT = (((1,), (1,)), ((), ()))


def _dot(a, b):
    return jnp.dot(a, b, preferred_element_type=jnp.float32)


def _dot_nt(a, b):
    return lax.dot_general(a, b, _NT, preferred_element_type=jnp.float32)


def _silu(g):
    return g * jax.nn.sigmoid(g)


def _norm_proj_kernel(x_ref, g_ref, w_ref, qg_ref, kg_ref, qkv_ref, gate_ref, grp_ref):
    @pl.when(pl.program_id(0) == 0)
    def _():
        r = lax.broadcasted_iota(jnp.int32, (SB_WIDTH // 2, SB_WIDTH // 2), 0) // HEAD_DIM
        c = lax.broadcasted_iota(jnp.int32, (SB_WIDTH // 2, SB_WIDTH // 2), 1) // HEAD_DIM
        grp_ref[...] = jnp.where(r == c, 1.0, 0.0).astype(jnp.bfloat16)

    x = x_ref[...]
    ms = jnp.mean(x * x, axis=-1, keepdims=True)
    h = (x * lax.rsqrt(ms + RMS_EPS) * g_ref[...]).astype(jnp.bfloat16)

    def chunk(c):
        return _dot(h, w_ref[0, :, c * SB_WIDTH:(c + 1) * SB_WIDTH])

    def head_norm(y, gain):
        y2 = (y * y).astype(jnp.bfloat16)
        half = SB_WIDTH // 2
        ss = jnp.concatenate([_dot(y2[:, 0:half], grp_ref[...]), _dot(y2[:, half:], grp_ref[...])], axis=1)
        return y * lax.rsqrt(ss * (1.0 / HEAD_DIM) + RMS_EPS) * gain

    qkv_ref[:, 0 * SB_WIDTH:1 * SB_WIDTH] = (chunk(0) * (QK_SCALE * LOG2E)).astype(jnp.bfloat16)
    qkv_ref[:, 1 * SB_WIDTH:2 * SB_WIDTH] = chunk(1).astype(jnp.bfloat16)
    qkv_ref[:, 2 * SB_WIDTH:3 * SB_WIDTH] = chunk(2).astype(jnp.bfloat16)
    gate_ref[:, 0:SB_WIDTH] = chunk(3)
    qkv_ref[:, 3 * SB_WIDTH:4 * SB_WIDTH] = head_norm(chunk(4), qg_ref[...]).astype(jnp.bfloat16)
    qkv_ref[:, 4 * SB_WIDTH:5 * SB_WIDTH] = head_norm(chunk(5), kg_ref[...]).astype(jnp.bfloat16)
    qkv_ref[:, 5 * SB_WIDTH:6 * SB_WIDTH] = chunk(6).astype(jnp.bfloat16)
    gate_ref[:, SB_WIDTH:2 * SB_WIDTH] = chunk(7)


def _norm_proj(x2d, norm_g, w_bf16, layer, qg_t, kg_t):
    m = x2d.shape[0]
    return pl.pallas_call(
        _norm_proj_kernel,
        grid=(m // TM_PROJ,),
        in_specs=[
            pl.BlockSpec((TM_PROJ, D_MODEL), lambda i: (i, 0)),
            pl.BlockSpec((1, D_MODEL), lambda i: (0, 0)),
            pl.BlockSpec((1, D_MODEL, PROJ_WIDTH), lambda i: (layer, 0, 0)),
            pl.BlockSpec((1, SB_WIDTH), lambda i: (0, 0)),
            pl.BlockSpec((1, SB_WIDTH), lambda i: (0, 0)),
        ],
        out_specs=[
            pl.BlockSpec((TM_PROJ, 6 * SB_WIDTH), lambda i: (i, 0)),
            pl.BlockSpec((TM_PROJ, 2 * SB_WIDTH), lambda i: (i, 0)),
        ],
        out_shape=[
            jax.ShapeDtypeStruct((m, 6 * SB_WIDTH), jnp.bfloat16),
            jax.ShapeDtypeStruct((m, 2 * SB_WIDTH), jnp.float32),
        ],
        scratch_shapes=[pltpu.VMEM((SB_WIDTH // 2, SB_WIDTH // 2), jnp.bfloat16)],
        compiler_params=pltpu.CompilerParams(
            dimension_semantics=("arbitrary",), vmem_limit_bytes=VMEM_LIMIT),
        name="norm_proj",
    )(x2d, norm_g, w_bf16, qg_t, kg_t)


def _stack_halves(q, lane):
    zero = jnp.zeros_like(q)
    return jnp.concatenate([jnp.where(lane < HEAD_DIM, q, zero), jnp.where(lane >= HEAD_DIM, q, zero)], axis=0)


def _sb_kernel(q_ref, k_ref, v_ref, g_ref, da_ref, x_ref, wsb_ref, wda_ref, o_ref,
               ntri_ref, acc_ref, car_ref, *, seq):
    nq = seq // TQ

    @pl.when((pl.program_id(0) == 0) & (pl.program_id(1) == 0))
    def _():
        r = lax.broadcasted_iota(jnp.int32, (TK, TK), 0)
        c = lax.broadcasted_iota(jnp.int32, (TK, TK), 1)
        ntri_ref[...] = jnp.where(r > c, -1.0, 0.0).astype(jnp.bfloat16)

    lane = lax.broadcasted_iota(jnp.int32, (1, LANES), 1)
    row = lax.broadcasted_iota(jnp.int32, (2 * TQ, TK), 0) & (TQ - 1)
    col = lax.broadcasted_iota(jnp.int32, (2 * TQ, TK), 1)
    mask = col < row

    def stacked_q(qi):
        return _stack_halves(q_ref[0, qi * TQ:(qi + 1) * TQ, :], lane)

    def key_blocks(q2, first, last, acc, car, diagonal_first):
        for j in range(first, last - 1, -1):
            masked = diagonal_first and j == first
            keys = slice(j * TK, (j + 1) * TK)
            z = _dot_nt(q2, k_ref[0, keys, :])
            sp = jnp.maximum(z, 0.0) + jnp.log(1.0 + jnp.exp2(-jnp.abs(z))) * LOG2E
            if masked:
                sp = jnp.where(mask, sp, 0.0)
            tail = _dot(sp.astype(jnp.bfloat16), ntri_ref[...])
            arg = (z - sp) + tail
            if car is not None:
                arg = arg + car
            a = jnp.exp2(arg)
            if masked:
                a = jnp.where(mask, a, 0.0)
            d = _dot(a.astype(jnp.bfloat16), v_ref[0, keys, :])
            acc = d if acc is None else acc + d
            rs = jnp.sum(sp, axis=-1, keepdims=True)
            car = -rs if car is None else car - rs
        return acc, car

    alive = {}
    for qi in range(nq):
        acc, car = key_blocks(stacked_q(qi), qi, max(qi - 1, 0), None, None, True)
        if qi > 1:
            alive[qi] = jnp.max(car) >= SB_DEAD_LOG2
            car_ref[qi] = car
        acc_ref[qi] = acc

    for qi in range(2, nq):
        @pl.when(alive[qi])
        def _(qi=qi):
            acc, _ = key_blocks(stacked_q(qi), qi - 2, 0, acc_ref[qi], car_ref[qi], False)
            acc_ref[qi] = acc

    def project(first):
        w_pair = jnp.concatenate([wsb_ref[0], wda_ref[0]], axis=0)
        for qi in range(nq):
            rows = slice(qi * TQ, (qi + 1) * TQ)
            out = jnp.where(lane < HEAD_DIM, acc_ref[qi, 0:TQ, :], acc_ref[qi, TQ:2 * TQ, :])
            sb = (out * _silu(g_ref[0, rows, :])).astype(jnp.bfloat16)
            y = _dot(jnp.concatenate([sb, da_ref[0, rows, :]], axis=1), w_pair)
            base = x_ref[0, rows, :] if first else o_ref[0, rows, :]
            o_ref[0, rows, :] = base + y

    @pl.when(pl.program_id(1) == 0)
    def _():
        project(True)

    @pl.when(pl.program_id(1) != 0)
    def _():
        project(False)


def _sb_attention_out(qkv, gates, da_mix, x, w_out_bf16, layer, batch, seq):
    hp = SB_WIDTH // LANES
    assert DA_HEADS == hp
    return pl.pallas_call(
        functools.partial(_sb_kernel, seq=seq),
        grid=(batch, hp),
        in_specs=[
            pl.BlockSpec((1, seq, LANES), lambda b, p: (b, 0, p)),
            pl.BlockSpec((1, seq, LANES), lambda b, p: (b, 0, hp + p)),
            pl.BlockSpec((1, seq, LANES), lambda b, p: (b, 0, 2 * hp + p)),
            pl.BlockSpec((1, seq, LANES), lambda b, p: (b, 0, p)),
            pl.BlockSpec((1, seq, LANES), lambda b, p: (b, 0, p)),
            pl.BlockSpec((1, seq, D_MODEL), lambda b, p: (b, 0, 0)),
            pl.BlockSpec((1, LANES, D_MODEL), lambda b, p: (layer, p, 0)),
            pl.BlockSpec((1, LANES, D_MODEL), lambda b, p: (layer, hp + p, 0)),
        ],
        out_specs=pl.BlockSpec((1, seq, D_MODEL), lambda b, p: (b, 0, 0)),
        out_shape=jax.ShapeDtypeStruct((batch, seq, D_MODEL), jnp.float32),
        scratch_shapes=[
            pltpu.VMEM((TK, TK), jnp.bfloat16),
            pltpu.VMEM((seq // TQ, 2 * TQ, LANES), jnp.float32),
            pltpu.VMEM((seq // TQ, 2 * TQ, 1), jnp.float32),
        ],
        compiler_params=pltpu.CompilerParams(
            dimension_semantics=("arbitrary", "arbitrary"), vmem_limit_bytes=VMEM_LIMIT),
        name="sb_attention_out",
    )(qkv, qkv, qkv, gates, da_mix, x, w_out_bf16, w_out_bf16)


_X_ROW, _X_Q0, _X_SHIFT, _X_COL, _X_K0 = 0, 1, 2, 3, 4


def _da_kernel(par_ref, q_ref, k_ref, v_ref, g_ref, lq1_ref, lk1_ref, lq2_ref, lk2_ref, sg_ref, o_ref,
               kx_ref, vx_ref, dc_ref, sacc_ref, m_ref, l_ref, *, lam_init, seq):
    nq = seq // TQ
    slope = par_ref[pl.program_id(0)]
    shift = par_ref[DA_HEADS]
    safe = par_ref[DA_HEADS + 1]
    lane = lax.broadcasted_iota(jnp.int32, (1, LANES), 1)
    scale = jnp.asarray(QK_SCALE, jnp.bfloat16)

    def finish(bb, rows, o):
        lam = (jnp.exp(jnp.sum(lq1_ref[...] * lk1_ref[...], axis=-1, keepdims=True))
               - jnp.exp(jnp.sum(lq2_ref[...] * lk2_ref[...], axis=-1, keepdims=True)) + lam_init)
        d = o[0] - lam * o[1]
        d = d * lax.rsqrt(jnp.mean(d * d, axis=-1, keepdims=True) + RMS_EPS) * sg_ref[...]
        d = d * (1.0 - lam_init)
        o_ref[bb, rows, :] = (d * _silu(g_ref[bb, rows, :])).astype(o_ref.dtype)

    def fixed_shift(skip_far):
        pos = lax.broadcasted_iota(jnp.int32, (seq, 1), 0)
        colf = (pos & (TK - 1)).astype(jnp.float32)
        blkf = (pos - (pos & (TK - 1))).astype(jnp.float32)
        kx = jnp.where(lane <= _X_SHIFT, 1.0,
                       jnp.where(lane == _X_COL, slope * colf, jnp.where(lane == _X_K0, slope * blkf, 0.0)))
        for bb in range(DA_BATCH):
            kx_ref[bb, :, 0:LANES] = k_ref[bb]
            kx_ref[bb, :, LANES:2 * LANES] = kx.astype(jnp.bfloat16)
            vx_ref[bb, :, 0:LANES] = v_ref[bb]
            vx_ref[bb, :, LANES:2 * LANES] = jnp.ones((seq, LANES), jnp.bfloat16)
        r = lax.broadcasted_iota(jnp.int32, (2 * TQ, TK), 0) & (TQ - 1)
        c = lax.broadcasted_iota(jnp.int32, (2 * TQ, TK), 1)
        dc_ref[...] = jnp.where(c <= r, 0.0, jnp.where((c // CHUNK) == (r // CHUNK),
                                                       (-2.0 * slope) * (c - r).astype(jnp.float32), NEG_BIG))
        rowf = lax.broadcasted_iota(jnp.int32, (TQ, 1), 0).astype(jnp.float32)

        def stacked_q(bb, qi):
            qx = jnp.where(lane == _X_ROW, -slope * rowf,
                           jnp.where(lane == _X_Q0, -slope * float(qi * TQ),
                                     jnp.where(lane == _X_SHIFT, -shift,
                                               jnp.where((lane == _X_COL) | (lane == _X_K0), 1.0, 0.0))))
            qx = qx.astype(jnp.bfloat16)
            qs = _stack_halves(q_ref[bb, qi * TQ:(qi + 1) * TQ, :] * scale, lane)
            return jnp.concatenate([qs, jnp.concatenate([qx, qx], axis=0)], axis=1)

        def key_blocks(bb, q2, first, last, acc, diagonal_first):
            for j in range(first, last - 1, -1):
                keys = slice(j * TK, (j + 1) * TK)
                s = _dot_nt(q2, kx_ref[bb, keys, :])
                if diagonal_first and j == first:
                    s = s + dc_ref[...]
                p = jnp.exp(s).astype(jnp.bfloat16)
                d = _dot(p, vx_ref[bb, keys, :])
                acc = d if acc is None else acc + d
            return acc

        def normalised(acc):
            return (acc[0:TQ, 0:LANES] / acc[0:TQ, LANES:2 * LANES],
                    acc[TQ:2 * TQ, 0:LANES] / acc[TQ:2 * TQ, LANES:2 * LANES])

        for bb in range(DA_BATCH):
            for qi in range(nq):
                last = max(qi - DA_NEAR_BLOCKS, 0) if skip_far else 0
                acc = key_blocks(bb, stacked_q(bb, qi), qi, last, None, True)
                finish(bb, slice(qi * TQ, (qi + 1) * TQ), normalised(acc))

    far_dead = slope * float(DA_NEAR_BLOCKS * TK + 1) > -DA_DEAD_BIAS

    @pl.when((safe > 0.5) & far_dead)
    def _():
        fixed_shift(True)

    @pl.when((safe > 0.5) & jnp.logical_not(far_dead))
    def _():
        fixed_shift(False)

    @pl.when(safe <= 0.5)
    def _():
        row = lax.broadcasted_iota(jnp.int32, (TQ, TK), 0)
        col = lax.broadcasted_iota(jnp.int32, (TQ, TK), 1)

        def q_block(idx, carry):
            bb = idx // nq
            qi = idx - bb * nq
            q0 = pl.multiple_of(qi * TQ, TQ)
            q = q_ref[bb, pl.ds(q0, TQ), :] * scale
            zero = jnp.zeros_like(q)
            q_halves = (jnp.where(lane < HEAD_DIM, q, zero), jnp.where(lane >= HEAD_DIM, q, zero))
            sacc_ref[...] = jnp.zeros_like(sacc_ref)
            l_ref[...] = jnp.zeros_like(l_ref)
            m_ref[...] = jnp.full_like(m_ref, NEG_BIG)

            def block(j, diagonal):
                start = pl.multiple_of(j * TK, TK)
                k = k_ref[bb, pl.ds(start, TK), :]
                v = v_ref[bb, pl.ds(start, TK), :]
                dist = (row - col + (qi - j) * TK).astype(jnp.float32)
                bias = -slope * jnp.abs(dist)
                if diagonal:
                    bias = jnp.where((col // CHUNK) <= (row // CHUNK), bias, NEG_BIG)
                for c in range(2):
                    s = _dot_nt(q_halves[c], k) + bias
                    m_old = m_ref[c]
                    m_new = jnp.maximum(m_old, jnp.max(s, axis=-1, keepdims=True))
                    p = jnp.exp(s - m_new)
                    alpha = jnp.exp(m_old - m_new)
                    l_ref[c] = alpha * l_ref[c] + jnp.sum(p, axis=-1, keepdims=True)
                    sacc_ref[c] = alpha * sacc_ref[c] + _dot(p.astype(jnp.bfloat16), v)
                    m_ref[c] = m_new

            block(qi, True)

            def body(it, c):
                block(qi - 1 - it, False)
                return c

            lax.fori_loop(0, qi, body, 0)
            finish(bb, pl.ds(q0, TQ), (sacc_ref[0] / l_ref[0], sacc_ref[1] / l_ref[1]))
            return carry

        lax.fori_loop(0, DA_BATCH * nq, q_block, 0)


def _da_attention(params, qkv, gates, lq1, lk1, lq2, lk2, subln_g, batch, seq, lam_init):
    q_blk = 3 * SB_WIDTH // LANES
    k_blk = q_blk + DA_HEADS
    v_blk = k_blk + DA_HEADS
    g_blk = SB_WIDTH // LANES
    vec = pl.BlockSpec((1, HEAD_DIM), lambda h, b: (0, 0))
    return pl.pallas_call(
        functools.partial(_da_kernel, lam_init=lam_init, seq=seq),
        grid=(DA_HEADS, batch // DA_BATCH),
        in_specs=[
            pl.BlockSpec(memory_space=pltpu.SMEM),
            pl.BlockSpec((DA_BATCH, seq, LANES), lambda h, b: (b, 0, q_blk + h)),
            pl.BlockSpec((DA_BATCH, seq, LANES), lambda h, b: (b, 0, k_blk + h)),
            pl.BlockSpec((DA_BATCH, seq, LANES), lambda h, b: (b, 0, v_blk + h)),
            pl.BlockSpec((DA_BATCH, seq, LANES), lambda h, b: (b, 0, g_blk + h)),
            vec, vec, vec, vec,
            pl.BlockSpec((1, LANES), lambda h, b: (0, 0)),
        ],
        out_specs=pl.BlockSpec((DA_BATCH, seq, LANES), lambda h, b: (b, 0, h)),
        out_shape=jax.ShapeDtypeStruct((batch, seq, DA_WIDTH), jnp.bfloat16),
        scratch_shapes=[
            pltpu.VMEM((DA_BATCH, seq, 2 * LANES), jnp.bfloat16),
            pltpu.VMEM((DA_BATCH, seq, 2 * LANES), jnp.bfloat16),
            pltpu.VMEM((2 * TQ, TK), jnp.float32),
            pltpu.VMEM((2, TQ, LANES), jnp.float32),
            pltpu.VMEM((2, TQ, 1), jnp.float32),
            pltpu.VMEM((2, TQ, 1), jnp.float32),
        ],
        compiler_params=pltpu.CompilerParams(
            dimension_semantics=("arbitrary", "arbitrary"), vmem_limit_bytes=VMEM_LIMIT),
        name="da_attention",
    )(params, qkv, qkv, qkv, gates, lq1, lk1, lq2, lk2, subln_g)


def kernel(x, norm_g, w_in, w_out, q_norm_g, k_norm_g, lambda_q1, lambda_k1, lambda_q2, lambda_k2, subln_g):
    batch, seq, d_model = x.shape
    depth = norm_g.shape[0]
    assert d_model == D_MODEL and seq % TQ == 0 and (batch * seq) % TM_PROJ == 0 and batch % DA_BATCH == 0
    slopes = jnp.asarray([2.0 ** (-8.0 * (h + 1) / DA_HEADS) for h in range(DA_HEADS)], jnp.float32)
    w_in_bf16 = w_in.astype(jnp.bfloat16)
    w_out_bf16 = w_out.astype(jnp.bfloat16)
    for l in range(depth):
        x2d = x.reshape(batch * seq, d_model)
        lam_init = 0.8 - 0.6 * math.exp(-0.3 * l)
        reps = SB_WIDTH // HEAD_DIM
        shift = HEAD_DIM * QK_SCALE * jnp.max(jnp.abs(q_norm_g[l])) * jnp.max(jnp.abs(k_norm_g[l]))
        params = jnp.concatenate([slopes, jnp.stack([shift, (shift < DA_SAFE_SHIFT).astype(jnp.float32)])])
        qkv, gates = _norm_proj(
            x2d, norm_g[l][None, :], w_in_bf16, l,
            jnp.tile(q_norm_g[l], reps)[None, :], jnp.tile(k_norm_g[l], reps)[None, :])
        qkv = qkv.reshape(batch, seq, -1)
        gates = gates.reshape(batch, seq, -1)
        da_mix = _da_attention(
            params, qkv, gates, lambda_q1[l][None, :], lambda_k1[l][None, :],
            lambda_q2[l][None, :], lambda_k2[l][None, :], subln_g[l][None, :], batch, seq, lam_init)
        x = _sb_attention_out(qkv, gates, da_mix, x, w_out_bf16, l, batch, seq)
    return x
```

```python
import functools
import math

import jax
import jax.numpy as jnp
from jax import lax
from jax.experimental import pallas as pl
from jax.experimental.pallas import tpu as pltpu

D_MODEL = 1024
CHUNK = 64
HEAD_DIM = 64
SB_WIDTH = 512
DA_HEADS = 4
DA_WIDTH = 512
PROJ_WIDTH = 4096
RMS_EPS = 1e-6
LANES = 128
NEG_BIG = -1e30
LOG2E = 1.4426950408889634
QK_SCALE = HEAD_DIM ** -0.5

TM_PROJ = 512
TQ = 256
TK = 256
VMEM_LIMIT = 56 * 1024 * 1024
DA_SAFE_SHIFT = 40.0
SB_DEAD_LOG2 = -151.0
DA_DEAD_BIAS = -110.0
DA_NEAR_BLOCKS = 2
DA_BATCH = 2

_NT = (((1,), (1,)), ((), ()))


def _dot(a, b):
    return jnp.dot(a, b, preferred_element_type=jnp.float32)


def _dot_nt(a, b):
    return lax.dot_general(a, b, _NT, preferred_element_type=jnp.float32)


def _silu(g):
    return g * jax.nn.sigmoid(g)


def _norm_proj_kernel(x_ref, xn_ref, g_ref, w_ref, qg_ref, kg_ref, qkv_ref, gate_ref, grp_ref, h_ref, sbq_ref):
    def rms_normed(x):
        ms = jnp.mean(x * x, axis=-1, keepdims=True)
        return (x * lax.rsqrt(ms + RMS_EPS) * g_ref[...]).astype(jnp.bfloat16)

    def first_chunk(hh):
        return (_dot(hh, w_ref[0, :, 0:SB_WIDTH]) * (QK_SCALE * LOG2E)).astype(jnp.bfloat16)

    @pl.when(pl.program_id(0) == 0)
    def _():
        r = lax.broadcasted_iota(jnp.int32, (SB_WIDTH // 2, SB_WIDTH // 2), 0) // HEAD_DIM
        c = lax.broadcasted_iota(jnp.int32, (SB_WIDTH // 2, SB_WIDTH // 2), 1) // HEAD_DIM
        grp_ref[...] = jnp.where(r == c, 1.0, 0.0).astype(jnp.bfloat16)
        h0 = rms_normed(x_ref[...])
        h_ref[...] = h0
        sbq_ref[...] = first_chunk(h0)

    h = h_ref[...]
    qkv_ref[:, 0 * SB_WIDTH:1 * SB_WIDTH] = sbq_ref[...]

    def chunk(c):
        return _dot(h, w_ref[0, :, c * SB_WIDTH:(c + 1) * SB_WIDTH])

    def head_norm(y, gain):
        y2 = (y * y).astype(jnp.bfloat16)
        half = SB_WIDTH // 2
        ss = jnp.concatenate([_dot(y2[:, 0:half], grp_ref[...]), _dot(y2[:, half:], grp_ref[...])], axis=1)
        return y * lax.rsqrt(ss * (1.0 / HEAD_DIM) + RMS_EPS) * gain

    qkv_ref[:, 1 * SB_WIDTH:2 * SB_WIDTH] = chunk(1).astype(jnp.bfloat16)
    qkv_ref[:, 2 * SB_WIDTH:3 * SB_WIDTH] = chunk(2).astype(jnp.bfloat16)
    gate_ref[:, 0:SB_WIDTH] = chunk(3)
    qkv_ref[:, 3 * SB_WIDTH:4 * SB_WIDTH] = head_norm(chunk(4), qg_ref[...]).astype(jnp.bfloat16)
    qkv_ref[:, 4 * SB_WIDTH:5 * SB_WIDTH] = head_norm(chunk(5), kg_ref[...]).astype(jnp.bfloat16)
    qkv_ref[:, 5 * SB_WIDTH:6 * SB_WIDTH] = chunk(6).astype(jnp.bfloat16)
    gate_ref[:, SB_WIDTH:2 * SB_WIDTH] = chunk(7)

    h_next = rms_normed(xn_ref[...])
    sbq_ref[...] = first_chunk(h_next)
    h_ref[...] = h_next


def _norm_proj(x2d, norm_g, w_bf16, layer, qg_t, kg_t):
    m = x2d.shape[0]
    last = m // TM_PROJ - 1
    return pl.pallas_call(
        _norm_proj_kernel,
        grid=(m // TM_PROJ,),
        in_specs=[
            pl.BlockSpec((TM_PROJ, D_MODEL), lambda i: (i, 0)),
            pl.BlockSpec((TM_PROJ, D_MODEL), lambda i: (jnp.minimum(i + 1, last), 0)),
            pl.BlockSpec((1, D_MODEL), lambda i: (0, 0)),
            pl.BlockSpec((1, D_MODEL, PROJ_WIDTH), lambda i: (layer, 0, 0)),
            pl.BlockSpec((1, SB_WIDTH), lambda i: (0, 0)),
            pl.BlockSpec((1, SB_WIDTH), lambda i: (0, 0)),
        ],
        out_specs=[
            pl.BlockSpec((TM_PROJ, 6 * SB_WIDTH), lambda i: (i, 0)),
            pl.BlockSpec((TM_PROJ, 2 * SB_WIDTH), lambda i: (i, 0)),
        ],
        out_shape=[
            jax.ShapeDtypeStruct((m, 6 * SB_WIDTH), jnp.bfloat16),
            jax.ShapeDtypeStruct((m, 2 * SB_WIDTH), jnp.float32),
        ],
        scratch_shapes=[
            pltpu.VMEM((SB_WIDTH // 2, SB_WIDTH // 2), jnp.bfloat16),
            pltpu.VMEM((TM_PROJ, D_MODEL), jnp.bfloat16),
            pltpu.VMEM((TM_PROJ, SB_WIDTH), jnp.bfloat16),
        ],
        compiler_params=pltpu.CompilerParams(
            dimension_semantics=("arbitrary",), vmem_limit_bytes=VMEM_LIMIT),
        name="norm_proj",
    )(x2d, x2d, norm_g, w_bf16, qg_t, kg_t)


def _stack_halves(q, lane):
    zero = jnp.zeros_like(q)
    return jnp.concatenate([jnp.where(lane < HEAD_DIM, q, zero), jnp.where(lane >= HEAD_DIM, q, zero)], axis=0)


def _sb_kernel(q_ref, k_ref, v_ref, g_ref, da_ref, x_ref, wsb_ref, wda_ref, o_ref,
               ntri_ref, acc_ref, car_ref, *, seq):
    nq = seq // TQ

    @pl.when((pl.program_id(0) == 0) & (pl.program_id(1) == 0))
    def _():
        r = lax.broadcasted_iota(jnp.int32, (TK, TK), 0)
        c = lax.broadcasted_iota(jnp.int32, (TK, TK), 1)
        ntri_ref[...] = jnp.where(r > c, -1.0, 0.0).astype(jnp.bfloat16)

    lane = lax.broadcasted_iota(jnp.int32, (1, LANES), 1)
    row = lax.broadcasted_iota(jnp.int32, (2 * TQ, TK), 0) & (TQ - 1)
    col = lax.broadcasted_iota(jnp.int32, (2 * TQ, TK), 1)
    mask = col < row

    def stacked_q(qi):
        return _stack_halves(q_ref[0, qi * TQ:(qi + 1) * TQ, :], lane)

    def key_blocks(q2, first, last, acc, car, diagonal_first):
        for j in range(first, last - 1, -1):
            masked = diagonal_first and j == first
            keys = slice(j * TK, (j + 1) * TK)
            z = _dot_nt(q2, k_ref[0, keys, :])
            sp = jnp.maximum(z, 0.0) + jnp.log(1.0 + jnp.exp2(-jnp.abs(z))) * LOG2E
            if masked:
                sp = jnp.where(mask, sp, 0.0)
            tail = _dot(sp.astype(jnp.bfloat16), ntri_ref[...])
            arg = (z - sp) + tail
            if car is not None:
                arg = arg + car
            a = jnp.exp2(arg)
            if masked:
                a = jnp.where(mask, a, 0.0)
            d = _dot(a.astype(jnp.bfloat16), v_ref[0, keys, :])
            acc = d if acc is None else acc + d
            rs = jnp.sum(sp, axis=-1, keepdims=True)
            car = -rs if car is None else car - rs
        return acc, car

    alive = {}
    for qi in range(nq):
        acc, car = key_blocks(stacked_q(qi), qi, max(qi - 1, 0), None, None, True)
        if qi > 1:
            alive[qi] = jnp.max(car) >= SB_DEAD_LOG2
            car_ref[qi] = car
        acc_ref[qi] = acc

    for qi in range(2, nq):
        @pl.when(alive[qi])
        def _(qi=qi):
            acc, _ = key_blocks(stacked_q(qi), qi - 2, 0, acc_ref[qi], car_ref[qi], False)
            acc_ref[qi] = acc

    def project(first):
        w_pair = jnp.concatenate([wsb_ref[0], wda_ref[0]], axis=0)
        for qi in range(nq):
            rows = slice(qi * TQ, (qi + 1) * TQ)
            out = jnp.where(lane < HEAD_DIM, acc_ref[qi, 0:TQ, :], acc_ref[qi, TQ:2 * TQ, :])
            sb = (out * _silu(g_ref[0, rows, :])).astype(jnp.bfloat16)
            y = _dot(jnp.concatenate([sb, da_ref[0, rows, :]], axis=1), w_pair)
            base = x_ref[0, rows, :] if first else o_ref[0, rows, :]
            o_ref[0, rows, :] = base + y

    @pl.when(pl.program_id(1) == 0)
    def _():
        project(True)

    @pl.when(pl.program_id(1) != 0)
    def _():
        project(False)


def _sb_attention_out(qkv, gates, da_mix, x, w_out_bf16, layer, batch, seq):
    hp = SB_WIDTH // LANES
    assert DA_HEADS == hp
    return pl.pallas_call(
        functools.partial(_sb_kernel, seq=seq),
        grid=(batch, hp),
        in_specs=[
            pl.BlockSpec((1, seq, LANES), lambda b, p: (b, 0, p)),
            pl.BlockSpec((1, seq, LANES), lambda b, p: (b, 0, hp + p)),
            pl.BlockSpec((1, seq, LANES), lambda b, p: (b, 0, 2 * hp + p)),
            pl.BlockSpec((1, seq, LANES), lambda b, p: (b, 0, p)),
            pl.BlockSpec((1, seq, LANES), lambda b, p: (b, 0, p)),
            pl.BlockSpec((1, seq, D_MODEL), lambda b, p: (b, 0, 0)),
            pl.BlockSpec((1, LANES, D_MODEL), lambda b, p: (layer, p, 0)),
            pl.BlockSpec((1, LANES, D_MODEL), lambda b, p: (layer, hp + p, 0)),
        ],
        out_specs=pl.BlockSpec((1, seq, D_MODEL), lambda b, p: (b, 0, 0)),
        out_shape=jax.ShapeDtypeStruct((batch, seq, D_MODEL), jnp.float32),
        scratch_shapes=[
            pltpu.VMEM((TK, TK), jnp.bfloat16),
            pltpu.VMEM((seq // TQ, 2 * TQ, LANES), jnp.float32),
            pltpu.VMEM((seq // TQ, 2 * TQ, 1), jnp.float32),
        ],
        compiler_params=pltpu.CompilerParams(
            dimension_semantics=("arbitrary", "arbitrary"), vmem_limit_bytes=VMEM_LIMIT),
        name="sb_attention_out",
    )(qkv, qkv, qkv, gates, da_mix, x, w_out_bf16, w_out_bf16)


_X_ROW, _X_Q0, _X_SHIFT, _X_COL, _X_K0 = 0, 1, 2, 3, 4


def _da_kernel(par_ref, q_ref, k_ref, v_ref, g_ref, lq1_ref, lk1_ref, lq2_ref, lk2_ref, sg_ref, o_ref,
               kx_ref, vx_ref, dc_ref, sacc_ref, m_ref, l_ref, *, lam_init, seq):
    nq = seq // TQ
    slope = par_ref[pl.program_id(1)]
    shift = par_ref[DA_HEADS]
    safe = par_ref[DA_HEADS + 1]
    lane = lax.broadcasted_iota(jnp.int32, (1, LANES), 1)
    scale = jnp.asarray(QK_SCALE, jnp.bfloat16)

    def finish(bb, rows, o):
        lam = (jnp.exp(jnp.sum(lq1_ref[...] * lk1_ref[...], axis=-1, keepdims=True))
               - jnp.exp(jnp.sum(lq2_ref[...] * lk2_ref[...], axis=-1, keepdims=True)) + lam_init)
        d = o[0] - lam * o[1]
        d = d * lax.rsqrt(jnp.mean(d * d, axis=-1, keepdims=True) + RMS_EPS) * sg_ref[...]
        d = d * (1.0 - lam_init)
        o_ref[bb, rows, :] = (d * _silu(g_ref[bb, rows, :])).astype(o_ref.dtype)

    def fixed_shift(skip_far):
        pos = lax.broadcasted_iota(jnp.int32, (seq, 1), 0)
        colf = (pos & (TK - 1)).astype(jnp.float32)
        blkf = (pos - (pos & (TK - 1))).astype(jnp.float32)
        kx = jnp.where(lane <= _X_SHIFT, 1.0,
                       jnp.where(lane == _X_COL, slope * colf, jnp.where(lane == _X_K0, slope * blkf, 0.0)))
        for bb in range(DA_BATCH):
            kx_ref[bb, :, 0:LANES] = k_ref[bb]
            kx_ref[bb, :, LANES:2 * LANES] = kx.astype(jnp.bfloat16)
            vx_ref[bb, :, 0:LANES] = v_ref[bb]
            vx_ref[bb, :, LANES:2 * LANES] = jnp.ones((seq, LANES), jnp.bfloat16)
        r = lax.broadcasted_iota(jnp.int32, (2 * TQ, TK), 0) & (TQ - 1)
        c = lax.broadcasted_iota(jnp.int32, (2 * TQ, TK), 1)
        dc_ref[...] = jnp.where(c <= r, 0.0, jnp.where((c // CHUNK) == (r // CHUNK),
                                                       (-2.0 * slope) * (c - r).astype(jnp.float32), NEG_BIG))
        rowf = lax.broadcasted_iota(jnp.int32, (TQ, 1), 0).astype(jnp.float32)

        def stacked_q(bb, qi):
            qx = jnp.where(lane == _X_ROW, -slope * rowf,
                           jnp.where(lane == _X_Q0, -slope * float(qi * TQ),
                                     jnp.where(lane == _X_SHIFT, -shift,
                                               jnp.where((lane == _X_COL) | (lane == _X_K0), 1.0, 0.0))))
            qx = qx.astype(jnp.bfloat16)
            qs = _stack_halves(q_ref[bb, qi * TQ:(qi + 1) * TQ, :] * scale, lane)
            return jnp.concatenate([qs, jnp.concatenate([qx, qx], axis=0)], axis=1)

        def key_blocks(bb, q2, first, last, acc, diagonal_first):
            for j in range(first, last - 1, -1):
                keys = slice(j * TK, (j + 1) * TK)
                s = _dot_nt(q2, kx_ref[bb, keys, :])
                if diagonal_first and j == first:
                    s = s + dc_ref[...]
                p = jnp.exp(s).astype(jnp.bfloat16)
                d = _dot(p, vx_ref[bb, keys, :])
                acc = d if acc is None else acc + d
            return acc

        def normalised(acc):
            return (acc[0:TQ, 0:LANES] / acc[0:TQ, LANES:2 * LANES],
                    acc[TQ:2 * TQ, 0:LANES] / acc[TQ:2 * TQ, LANES:2 * LANES])

        for bb in range(DA_BATCH):
            for qi in range(nq):
                last = max(qi - DA_NEAR_BLOCKS, 0) if skip_far else 0
                acc = key_blocks(bb, stacked_q(bb, qi), qi, last, None, True)
                finish(bb, slice(qi * TQ, (qi + 1) * TQ), normalised(acc))

    far_dead = slope * float(DA_NEAR_BLOCKS * TK + 1) > -DA_DEAD_BIAS

    @pl.when((safe > 0.5) & far_dead)
    def _():
        fixed_shift(True)

    @pl.when((safe > 0.5) & jnp.logical_not(far_dead))
    def _():
        fixed_shift(False)

    @pl.when(safe <= 0.5)
    def _():
        row = lax.broadcasted_iota(jnp.int32, (TQ, TK), 0)
        col = lax.broadcasted_iota(jnp.int32, (TQ, TK), 1)

        def q_block(idx, carry):
            bb = idx // nq
            qi = idx - bb * nq
            q0 = pl.multiple_of(qi * TQ, TQ)
            q = q_ref[bb, pl.ds(q0, TQ), :] * scale
            zero = jnp.zeros_like(q)
            q_halves = (jnp.where(lane < HEAD_DIM, q, zero), jnp.where(lane >= HEAD_DIM, q, zero))
            sacc_ref[...] = jnp.zeros_like(sacc_ref)
            l_ref[...] = jnp.zeros_like(l_ref)
            m_ref[...] = jnp.full_like(m_ref, NEG_BIG)

            def block(j, diagonal):
                start = pl.multiple_of(j * TK, TK)
                k = k_ref[bb, pl.ds(start, TK), :]
                v = v_ref[bb, pl.ds(start, TK), :]
                dist = (row - col + (qi - j) * TK).astype(jnp.float32)
                bias = -slope * jnp.abs(dist)
                if diagonal:
                    bias = jnp.where((col // CHUNK) <= (row // CHUNK), bias, NEG_BIG)
                for c in range(2):
                    s = _dot_nt(q_halves[c], k) + bias
                    m_old = m_ref[c]
                    m_new = jnp.maximum(m_old, jnp.max(s, axis=-1, keepdims=True))
                    p = jnp.exp(s - m_new)
                    alpha = jnp.exp(m_old - m_new)
                    l_ref[c] = alpha * l_ref[c] + jnp.sum(p, axis=-1, keepdims=True)
                    sacc_ref[c] = alpha * sacc_ref[c] + _dot(p.astype(jnp.bfloat16), v)
                    m_ref[c] = m_new

            block(qi, True)

            def body(it, c):
                block(qi - 1 - it, False)
                return c

            lax.fori_loop(0, qi, body, 0)
            finish(bb, pl.ds(q0, TQ), (sacc_ref[0] / l_ref[0], sacc_ref[1] / l_ref[1]))
            return carry

        lax.fori_loop(0, DA_BATCH * nq, q_block, 0)


def _da_attention(params, qkv, gates, lq1, lk1, lq2, lk2, subln_g, batch, seq, lam_init):
    q_blk = 3 * SB_WIDTH // LANES
    k_blk = q_blk + DA_HEADS
    v_blk = k_blk + DA_HEADS
    g_blk = SB_WIDTH // LANES
    vec = pl.BlockSpec((1, HEAD_DIM), lambda b, h: (0, 0))
    return pl.pallas_call(
        functools.partial(_da_kernel, lam_init=lam_init, seq=seq),
        grid=(batch // DA_BATCH, DA_HEADS),
        in_specs=[
            pl.BlockSpec(memory_space=pltpu.SMEM),
            pl.BlockSpec((DA_BATCH, seq, LANES), lambda b, h: (b, 0, q_blk + h)),
            pl.BlockSpec((DA_BATCH, seq, LANES), lambda b, h: (b, 0, k_blk + h)),
            pl.BlockSpec((DA_BATCH, seq, LANES), lambda b, h: (b, 0, v_blk + h)),
            pl.BlockSpec((DA_BATCH, seq, LANES), lambda b, h: (b, 0, g_blk + h)),
            vec, vec, vec, vec,
            pl.BlockSpec((1, LANES), lambda b, h: (0, 0)),
        ],
        out_specs=pl.BlockSpec((DA_BATCH, seq, LANES), lambda b, h: (b, 0, h)),
        out_shape=jax.ShapeDtypeStruct((batch, seq, DA_WIDTH), jnp.bfloat16),
        scratch_shapes=[
            pltpu.VMEM((DA_BATCH, seq, 2 * LANES), jnp.bfloat16),
            pltpu.VMEM((DA_BATCH, seq, 2 * LANES), jnp.bfloat16),
            pltpu.VMEM((2 * TQ, TK), jnp.float32),
            pltpu.VMEM((2, TQ, LANES), jnp.float32),
            pltpu.VMEM((2, TQ, 1), jnp.float32),
            pltpu.VMEM((2, TQ, 1), jnp.float32),
        ],
        compiler_params=pltpu.CompilerParams(
            dimension_semantics=("arbitrary", "arbitrary"), vmem_limit_bytes=VMEM_LIMIT),
        name="da_attention",
    )(params, qkv, qkv, qkv, gates, lq1, lk1, lq2, lk2, subln_g)


def kernel(x, norm_g, w_in, w_out, q_norm_g, k_norm_g, lambda_q1, lambda_k1, lambda_q2, lambda_k2, subln_g):
    batch, seq, d_model = x.shape
    depth = norm_g.shape[0]
    assert d_model == D_MODEL and seq % TQ == 0 and (batch * seq) % TM_PROJ == 0 and batch % DA_BATCH == 0
    slopes = jnp.asarray([2.0 ** (-8.0 * (h + 1) / DA_HEADS) for h in range(DA_HEADS)], jnp.float32)
    w_in_bf16 = w_in.astype(jnp.bfloat16)
    w_out_bf16 = w_out.astype(jnp.bfloat16)
    for l in range(depth):
        x2d = x.reshape(batch * seq, d_model)
        lam_init = 0.8 - 0.6 * math.exp(-0.3 * l)
        reps = SB_WIDTH // HEAD_DIM
        shift = HEAD_DIM * QK_SCALE * jnp.max(jnp.abs(q_norm_g[l])) * jnp.max(jnp.abs(k_norm_g[l]))
        params = jnp.concatenate([slopes, jnp.stack([shift, (shift < DA_SAFE_SHIFT).astype(jnp.float32)])])
        qkv, gates = _norm_proj(
            x2d, norm_g[l][None, :], w_in_bf16, l,
            jnp.tile(q_norm_g[l], reps)[None, :], jnp.tile(k_norm_g[l], reps)[None, :])
        qkv = qkv.reshape(batch, seq, -1)
        gates = gates.reshape(batch, seq, -1)
        da_mix = _da_attention(
            params, qkv, gates, lambda_q1[l][None, :], lambda_k1[l][None, :],
            lambda_q2[l][None, :], lambda_k2[l][None, :], subln_g[l][None, :], batch, seq, lam_init)
        x = _sb_attention_out(qkv, gates, da_mix, x, w_out_bf16, l, batch, seq)
    return x
```
